```python
import numpy as np
import jax, jax.numpy as jnp
from jax import lax

D_MODEL = 1024
BATCH = 4
SEQ = 4096
DEPTH = 2
DEC_BATCH = 128
DEC_SEQ = 8
PAST_LEN = 2048
PAGE_SIZE = 128

GLA_HEADS = 4
GLA_DK = 64
GLA_DV = 128
GLA_LOWRANK = 16
GLA_TAU = 16.0
GLA_CHUNK = 16
SB_HEADS = 8
SB_DH = 64
FOX_HEADS = 8
FOX_DH = 64
Q_BLOCK = 128
D_FF = 2816
CONV_W = 3
NORM_EPS = 1e-6
FORGET_BIAS_INIT = 1.0
GLA_KW = GLA_HEADS * GLA_DK
GLA_VW = GLA_HEADS * GLA_DV
SB_W = SB_HEADS * SB_DH
FOX_W = FOX_HEADS * FOX_DH
IN_SIZES = (GLA_KW, GLA_KW, GLA_VW, GLA_LOWRANK, GLA_VW, SB_W, SB_W, SB_W, FOX_W, FOX_W, FOX_W, FOX_HEADS, 3 * D_MODEL)
D_IN = sum(IN_SIZES)

kernel_name = 'hybrid_gla_stickbreak_fox_convffn_step'


def _rmsnorm(x, g):
    xf = x.astype(jnp.float32)
    y = xf * lax.rsqrt(jnp.mean(xf * xf, axis=-1, keepdims=True) + NORM_EPS)
    return (y * g.astype(jnp.float32)).astype(x.dtype)


def _project(h, w_in, w_gla_lr, b_gla_lr, b_fox_f):
    B, L, _ = h.shape
    z = h @ w_in
    idx = np.cumsum(IN_SIZES)[:-1].tolist()
    gq, gk, gv, glr, gg, sq, sk, sv, fq, fk, fv, ff, gates = jnp.split(z, idx, axis=-1)
    heads = lambda t, n: t.reshape(B, L, n, -1)
    loga = jax.nn.log_sigmoid((glr @ w_gla_lr + b_gla_lr).astype(jnp.float32)) / GLA_TAU
    logf = jax.nn.log_sigmoid((ff + b_fox_f).astype(jnp.float32))
    return (heads(gq, GLA_HEADS), heads(gk, GLA_HEADS), heads(gv, GLA_HEADS), heads(loga, GLA_HEADS), gg,
            heads(sq, SB_HEADS), heads(sk, SB_HEADS), heads(sv, SB_HEADS),
            heads(fq, FOX_HEADS), heads(fk, FOX_HEADS), heads(fv, FOX_HEADS), logf, gates)


def _gla(q, k, v, loga, s0):
    B, L, H, DK = q.shape
    DV = v.shape[-1]
    C = GLA_CHUNK if L % GLA_CHUNK == 0 else L
    N = L // C
    f32 = jnp.float32
    q = q.astype(f32).reshape(B, N, C, H, DK) * DK ** -0.5
    k = k.astype(f32).reshape(B, N, C, H, DK)
    v = v.astype(f32).reshape(B, N, C, H, DV)
    b = lax.cumsum(loga.reshape(B, N, C, H, DK), axis=2)
    causal = jnp.tril(jnp.ones((C, C), bool))[None, None, :, :, None, None]
    decay = jnp.exp(jnp.where(causal, b[:, :, :, None] - b[:, :, None, :], -jnp.inf))
    a_intra = jnp.einsum('bnthd,bnshd,bntshd->bnhts', q, k, decay)
    o = jnp.einsum('bnhts,bnshv->bnthv', a_intra, v)
    b_end = b[:, :, -1]
    u = jnp.einsum('bnshd,bnshv->bnhdv', k * jnp.exp(b_end[:, :, None] - b), v)

    def step(s, inp):
        a_n, u_n = inp
        return a_n[..., None] * s + u_n, s

    s_final, s_start = lax.scan(step, s0.astype(f32),
                                (jnp.exp(b_end).transpose(1, 0, 2, 3), u.transpose(1, 0, 2, 3, 4)))
    o = o + jnp.einsum('bnthd,nbhdv->bnthv', q * jnp.exp(b), s_start)
    return o.reshape(B, L, H, DV), s_final.astype(s0.dtype)


def _gla_out(o, g, gate):
    B, L = o.shape[:2]
    return _rmsnorm(o, g).reshape(B, L, -1) * jax.nn.silu(gate.astype(jnp.float32))


def _stick_breaking(q, k, v, q_pos, k_pos):
    z = jnp.einsum('bqhd,bkhd->bhqk', q, k).astype(jnp.float32) * q.shape[-1] ** -0.5
    mask = k_pos[None, :] < q_pos[:, None]
    log_keep = jnp.where(mask, jax.nn.log_sigmoid(-z), 0.0)
    log_after = lax.cumsum(log_keep, axis=3, reverse=True) - log_keep
    w = jnp.where(mask, jnp.exp(jax.nn.log_sigmoid(z) + log_after), 0.0)
    return jnp.einsum('bhqk,bkhd->bqhd', w.astype(v.dtype), v)


def _forgetting(q, k, v, cq, ck, q_pos, k_pos):
    s = jnp.einsum('bqhd,bkhd->bhqk', q, k).astype(jnp.float32) * q.shape[-1] ** -0.5
    s = s + cq.transpose(0, 2, 1)[:, :, :, None] - ck.transpose(0, 2, 1)[:, :, None, :]
    s = jnp.where(k_pos[None, :] <= q_pos[:, None], s, -jnp.inf)
    p = jax.nn.softmax(s, axis=-1)
    return jnp.einsum('bhqk,bkhd->bqhd', p.astype(v.dtype), v)


def _merge(gates, o_gla, o_sb, o_fx, w_br_gla, w_br_sb, w_br_fox, w_o, dtype):
    g1, g2, g3 = jnp.split(jax.nn.sigmoid(gates.astype(jnp.float32)), 3, axis=-1)
    m = g1 * (o_gla.astype(dtype) @ w_br_gla) + g2 * (o_sb @ w_br_sb) + g3 * (o_fx @ w_br_fox)
    return (m.astype(dtype) @ w_o).astype(dtype)


def _conv_ffn(h, past_u, w_ffn_in, w_conv, b_conv, w_down):
    L = h.shape[1]
    u, g = jnp.split(h @ w_ffn_in, 2, axis=-1)
    ext = jnp.concatenate([past_u.astype(u.dtype), u], axis=1)
    uc = b_conv + sum(ext[:, i:i + L] * w_conv[i] for i in range(CONV_W))
    y = (jax.nn.silu(uc) * g) @ w_down
    return y.astype(h.dtype), ext[:, ext.shape[1] - (CONV_W - 1):]


def _blocks(L):
    return [(a, min(a + Q_BLOCK, L)) for a in range(0, L, Q_BLOCK)]


def _gather(cache_l, page_table):
    g = cache_l[page_table]
    return g.reshape((page_table.shape[0], -1) + cache_l.shape[2:])


def _layer(x, past_sb_k, past_sb_v, past_fx_k, past_fx_v, past_fx_logf, gla_s0, conv_past, lw):
    (g_mix, w_in, w_gla_lr, b_gla_lr, g_gla_out, b_fox_f, w_br_gla, w_br_sb, w_br_fox, w_o,
     g_ffn, w_ffn_in, w_conv, b_conv, w_down) = lw
    B, L, _ = x.shape
    P = past_sb_k.shape[1]
    h = _rmsnorm(x, g_mix)
    gq, gk, gv, loga, gg, sq, sk, sv, fq, fk, fv, logf, gates = _project(h, w_in, w_gla_lr, b_gla_lr, b_fox_f)
    o_gla, gla_s = _gla(gq, gk, gv, loga, gla_s0)
    o_gla = _gla_out(o_gla, g_gla_out, gg)
    sb_k = jnp.concatenate([past_sb_k.astype(sk.dtype), sk], axis=1)
    sb_v = jnp.concatenate([past_sb_v.astype(sv.dtype), sv], axis=1)
    fx_k = jnp.concatenate([past_fx_k.astype(fk.dtype), fk], axis=1)
    fx_v = jnp.concatenate([past_fx_v.astype(fv.dtype), fv], axis=1)
    c = lax.cumsum(jnp.concatenate([past_fx_logf.astype(jnp.float32), logf], axis=1), axis=1)
    k_pos = jnp.arange(P + L)
    o_sb, o_fx = [], []
    for a, e in _blocks(L):
        q_pos = k_pos[P + a:P + e]
        o_sb.append(_stick_breaking(sq[:, a:e], sb_k[:, :P + e], sb_v[:, :P + e], q_pos, k_pos[:P + e]))
        o_fx.append(_forgetting(fq[:, a:e], fx_k[:, :P + e], fx_v[:, :P + e],
                                c[:, P + a:P + e], c[:, :P + e], q_pos, k_pos[:P + e]))
    o_sb = jnp.concatenate(o_sb, axis=1).reshape(B, L, SB_W)
    o_fx = jnp.concatenate(o_fx, axis=1).reshape(B, L, FOX_W)
    x = x + _merge(gates, o_gla, o_sb, o_fx, w_br_gla, w_br_sb, w_br_fox, w_o, x.dtype)
    y, conv_new = _conv_ffn(_rmsnorm(x, g_ffn), conv_past, w_ffn_in, w_conv, b_conv, w_down)
    x = x + y
    return x, (gla_s, sk, sv, fk, fv, logf, conv_new)


def setup_inputs(seed: int = 0) -> dict:
    key = jax.random.key(seed)
    ks = jax.random.split(key, 32)
    n_pages = PAST_LEN // PAGE_SIZE
    n_used = DEC_BATCH * n_pages
    n_pool = n_used + n_used // 4
    nrm = lambda k, shape, scale: jax.random.normal(k, shape, jnp.float32) * scale
    return {
        'x_prompt': nrm(ks[0], (BATCH, SEQ, D_MODEL), 1.0),
        'x_sample': nrm(ks[1], (DEC_BATCH, DEC_SEQ, D_MODEL), 1.0),
        'state_gla': nrm(ks[2], (DEPTH, DEC_BATCH, GLA_HEADS, GLA_DK, GLA_DV), 1.0),
        'cache_sb_k': nrm(ks[3], (DEPTH, n_pool, PAGE_SIZE, SB_HEADS, SB_DH), 1.0),
        'cache_sb_v': nrm(ks[4], (DEPTH, n_pool, PAGE_SIZE, SB_HEADS, SB_DH), 1.0),
        'cache_fox_k': nrm(ks[5], (DEPTH, n_pool, PAGE_SIZE, FOX_HEADS, FOX_DH), 1.0),
        'cache_fox_v': nrm(ks[6], (DEPTH, n_pool, PAGE_SIZE, FOX_HEADS, FOX_DH), 1.0),
        'cache_fox_logf': jax.nn.log_sigmoid(FORGET_BIAS_INIT + nrm(ks[7], (DEPTH, n_pool, PAGE_SIZE, FOX_HEADS), 1.0)),
        'state_ffn_conv': nrm(ks[8], (DEPTH, DEC_BATCH, CONV_W - 1, D_FF), 1.0),
        'page_table': jax.random.permutation(ks[9], n_pool)[:n_used].reshape(DEC_BATCH, n_pages).astype(jnp.int32),
        'g_mix': 1.0 + nrm(ks[10], (DEPTH, D_MODEL), 0.02),
        'w_in': nrm(ks[11], (DEPTH, D_MODEL, D_IN), D_MODEL ** -0.5),
        'w_gla_lr': nrm(ks[12], (DEPTH, GLA_LOWRANK, GLA_KW), GLA_LOWRANK ** -0.5),
        'b_gla_lr': nrm(ks[13], (DEPTH, GLA_KW), 0.1),
        'g_gla_out': 1.0 + nrm(ks[14], (DEPTH, GLA_DV), 0.02),
        'b_fox_f': FORGET_BIAS_INIT + nrm(ks[15], (DEPTH, FOX_HEADS), 0.1),
        'w_br_gla': nrm(ks[16], (DEPTH, GLA_VW, D_MODEL), GLA_VW ** -0.5),
        'w_br_sb': nrm(ks[17], (DEPTH, SB_W, D_MODEL), SB_W ** -0.5),
        'w_br_fox': nrm(ks[18], (DEPTH, FOX_W, D_MODEL), FOX_W ** -0.5),
        'w_o': nrm(ks[19], (DEPTH, D_MODEL, D_MODEL), D_MODEL ** -0.5),
        'g_ffn': 1.0 + nrm(ks[20], (DEPTH, D_MODEL), 0.02),
        'w_ffn_in': nrm(ks[21], (DEPTH, D_MODEL, 2 * D_FF), D_MODEL ** -0.5),
        'w_conv': nrm(ks[22], (DEPTH, CONV_W, D_FF), CONV_W ** -0.5),
        'b_conv': nrm(ks[23], (DEPTH, D_FF), 0.02),
        'w_down': nrm(ks[24], (DEPTH, D_FF, D_MODEL), D_FF ** -0.5),
        'g_final': 1.0 + nrm(ks[25], (D_MODEL,), 0.02),
    }


def reference(x_prompt, x_sample, state_gla, cache_sb_k, cache_sb_v, cache_fox_k, cache_fox_v, cache_fox_logf,
              state_ffn_conv, page_table, g_mix, w_in, w_gla_lr, b_gla_lr, g_gla_out, b_fox_f, w_br_gla, w_br_sb,
              w_br_fox, w_o, g_ffn, w_ffn_in, w_conv, b_conv, w_down, g_final):
    Bp = x_prompt.shape[0]
    dt = x_prompt.dtype
    xp, xs = x_prompt, x_sample
    new_p, new_s = [], []
    for l in range(DEPTH):
        lw = (g_mix[l], w_in[l], w_gla_lr[l], b_gla_lr[l], g_gla_out[l], b_fox_f[l], w_br_gla[l], w_br_sb[l],
              w_br_fox[l], w_o[l], g_ffn[l], w_ffn_in[l], w_conv[l], b_conv[l], w_down[l])
        xp, st_p = _layer(xp,
                          jnp.zeros((Bp, 0, SB_HEADS, SB_DH), dt), jnp.zeros((Bp, 0, SB_HEADS, SB_DH), dt),
                          jnp.zeros((Bp, 0, FOX_HEADS, FOX_DH), dt), jnp.zeros((Bp, 0, FOX_HEADS, FOX_DH), dt),
                          jnp.zeros((Bp, 0, FOX_HEADS), jnp.float32),
                          jnp.zeros((Bp, GLA_HEADS, GLA_DK, GLA_DV), dt),
                          jnp.zeros((Bp, CONV_W - 1, D_FF), dt), lw)
        xs, st_s = _layer(xs,
                          _gather(cache_sb_k[l], page_table), _gather(cache_sb_v[l], page_table),
                          _gather(cache_fox_k[l], page_table), _gather(cache_fox_v[l], page_table),
                          _gather(cache_fox_logf[l], page_table),
                          state_gla[l], state_ffn_conv[l], lw)
        new_p.append(st_p)
        new_s.append(st_s)
    y_prompt = _rmsnorm(xp, g_final)
    y_sample = _rmsnorm(xs, g_final)
    gla_state_prompt = jnp.stack([s[0] for s in new_p])
    gla_state_sample = jnp.stack([s[0] for s in new_s])
    sb_k_prompt = jnp.stack([s[1] for s in new_p])
    sb_v_prompt = jnp.stack([s[2] for s in new_p])
    sb_k_sample = jnp.stack([s[1] for s in new_s])
    sb_v_sample = jnp.stack([s[2] for s in new_s])
    fox_k_prompt = jnp.stack([s[3] for s in new_p])
    fox_v_prompt = jnp.stack([s[4] for s in new_p])
    fox_logf_prompt = jnp.stack([s[5] for s in new_p])
    fox_k_sample = jnp.stack([s[3] for s in new_s])
    fox_v_sample = jnp.stack([s[4] for s in new_s])
    fox_logf_sample = jnp.stack([s[5] for s in new_s])
    ffn_conv_prompt = jnp.stack([s[6] for s in new_p])
    ffn_conv_sample = jnp.stack([s[6] for s in new_s])
    return (y_prompt, y_sample, gla_state_prompt, gla_state_sample, sb_k_prompt, sb_v_prompt, sb_k_sample,
            sb_v_sample, fox_k_prompt, fox_v_prompt, fox_logf_prompt, fox_k_sample, fox_v_sample, fox_logf_sample,
            ffn_conv_prompt, ffn_conv_sample)
```

```python
import functools

import numpy as np
import jax
import jax.numpy as jnp
from jax import lax
from jax.experimental import pallas as pl
from jax.experimental.pallas import tpu as pltpu

F32, BF16 = jnp.float32, jnp.bfloat16

GLA_HEADS, GLA_DK, GLA_DV, GLA_LOWRANK, GLA_TAU, GLA_CHUNK = 4, 64, 128, 16, 16.0, 16
ATT_HEADS, ATT_DH = 8, 64
NORM_EPS = 1e-6
CONV_W = 3
GLA_KW = GLA_HEADS * GLA_DK
GLA_VW = GLA_HEADS * GLA_DV
ATT_W = ATT_HEADS * ATT_DH
IN_NAMES = ("gq", "gk", "gv", "glr", "gg", "sq", "sk", "sv", "fq", "fk", "fv", "ff", "gates")

LANE = 128
SUBLANE = 8
VMEM_LIMIT_BYTES = 52 * 1024 * 1024

PACK_ORDER = ("gq", "gk", "gv", "gg", "sq", "sk", "sv", "fq", "fk", "fv", "gates")
TAIL_LR = 0
TAIL_FF = GLA_LOWRANK

PROJ_TM = 256
ATT_T = 256
GLA_TL = 256
MERGE_TM = 512
FFN_TM = 512
FFN_FB = 256
CUMSUM_TB = 512
DEC_NP = 4


def _in_sizes(d_model):
    return dict(gq=GLA_KW, gk=GLA_KW, gv=GLA_VW, glr=GLA_LOWRANK, gg=GLA_VW, sq=ATT_W, sk=ATT_W, sv=ATT_W,
                fq=ATT_W, fk=ATT_W, fv=ATT_W, ff=ATT_HEADS, gates=3 * d_model)


def _pack_layout(d_model):
    sizes = _in_sizes(d_model)
    off, lay = 0, {}
    for n in PACK_ORDER:
        lay[n] = (off, off + sizes[n])
        off += sizes[n]
    lay["tail"] = (off, off + LANE)
    return lay, off + LANE


def _dot(a, b):
    return jnp.dot(a, b, preferred_element_type=F32)


def _dot_nt(a, b):
    return lax.dot_general(a, b, (((1,), (1,)), ((), ())), preferred_element_type=F32)


def _split(a, n):
    parts, r = [], a
    for i in range(n):
        p = r.astype(BF16)
        parts.append(p)
        if i + 1 < n:
            r = r - p.astype(F32)
    return parts


def _split_dot_l(a, b_bf, n):
    out = None
    for p in _split(a, n):
        t = _dot(p, b_bf)
        out = t if out is None else out + t
    return out


def _split_dot_r(a_bf, b, n):
    out = None
    for p in _split(b, n):
        t = _dot(a_bf, p)
        out = t if out is None else out + t
    return out


def _log_sigmoid(x):
    return jnp.minimum(x, 0.0) - jnp.log1p(jnp.exp(-jnp.abs(x)))


def _sigmoid(x):
    return 1.0 / (1.0 + jnp.exp(-x))


def _rmsnorm(x, g):
    return x * lax.rsqrt(jnp.mean(x * x, axis=-1, keepdims=True) + NORM_EPS) * g


def _params(*sem):
    return pltpu.CompilerParams(dimension_semantics=sem, vmem_limit_bytes=VMEM_LIMIT_BYTES)


def _resident(shape):
    nd = len(shape)
    return pl.BlockSpec(shape, lambda *_: (0,) * nd, pipeline_mode=pl.Buffered(1))


def _proj_kernel(lay, x_ref, g_ref, w_ref, wlr_hi_ref, wlr_lo_ref, blr_ref, bff_ref, *out_refs):
    outs = dict(zip(PACK_ORDER + ("la", "tail"), out_refs))
    h = _rmsnorm(x_ref[...], g_ref[...]).astype(BF16)

    def seg(name):
        a, b = lay[name]
        return _dot(h, w_ref[:, a:b])

    scale = ATT_DH ** -0.5
    for n in PACK_ORDER:
        z = seg(n)
        outs[n][...] = z * scale if n in ("gq", "sq", "fq") else z
    tail = seg("tail")
    t_hi, t_lo = _split(tail, 2)
    xlr = _dot(t_hi, wlr_hi_ref[...]) + _dot(t_lo, wlr_hi_ref[...]) + _dot(t_hi, wlr_lo_ref[...])
    outs["la"][...] = _log_sigmoid(xlr + blr_ref[...]) * (1.0 / GLA_TAU)
    outs["tail"][...] = _log_sigmoid(tail + bff_ref[...])


def _project(x, g, w_packed, wlr_hi, wlr_lo, blr, bff):
    t, d = x.shape
    lay, width = _pack_layout(d)
    tm = min(PROJ_TM, t)
    assert t % tm == 0
    names = PACK_ORDER + ("la", "tail")
    widths = {n: lay[n][1] - lay[n][0] for n in PACK_ORDER}
    widths["la"] = GLA_KW
    widths["tail"] = LANE
    row = lambda w: pl.BlockSpec((tm, w), lambda i: (i, 0))
    outs = pl.pallas_call(
        functools.partial(_proj_kernel, lay),
        grid=(t // tm,),
        in_specs=[row(d), _resident((1, d)), _resident((d, width)), _resident((LANE, GLA_KW)),
                  _resident((LANE, GLA_KW)), _resident((1, GLA_KW)), _resident((1, LANE))],
        out_specs=[row(widths[n]) for n in names],
        out_shape=[jax.ShapeDtypeStruct((t, widths[n]), F32) for n in names],
        compiler_params=_params("arbitrary"),
        name="project",
    )(x, g, w_packed, wlr_hi, wlr_lo, blr, bff)
    return dict(zip(names, outs))


def _cumsum_kernel(x_ref, tri_ref, o_ref, carry_ref):
    @pl.when(pl.program_id(1) == 0)
    def _():
        carry_ref[...] = jnp.zeros_like(carry_ref)

    c = _split_dot_r(tri_ref[...], x_ref[...], 3) + carry_ref[...]
    o_ref[...] = c
    carry_ref[...] = c[-1:, :]


def _seq_cumsum(x, n_seq, seq_len):
    tb = min(CUMSUM_TB, seq_len)
    assert seq_len % tb == 0
    nt = seq_len // tb
    tri = jnp.asarray(np.tril(np.ones((tb, tb), np.float32)), BF16)
    return pl.pallas_call(
        _cumsum_kernel,
        grid=(n_seq, nt),
        in_specs=[pl.BlockSpec((tb, LANE), lambda b, i: (b * nt + i, 0)), _resident((tb, tb))],
        out_specs=pl.BlockSpec((tb, LANE), lambda b, i: (b * nt + i, 0)),
        out_shape=jax.ShapeDtypeStruct(x.shape, F32),
        scratch_shapes=[pltpu.VMEM((1, LANE), F32)],
        compiler_params=_params("arbitrary", "arbitrary"),
        name="logf_cumsum",
    )(x, tri)


def _tri_tables(nq):
    qi = [q for q in range(nq) for _ in range(q + 1)]
    kj = [k for q in range(nq) for k in range(q, -1, -1)]
    return jnp.asarray(qi, jnp.int32), jnp.asarray(kj, jnp.int32)


def _sb_prompt_kernel(qi_ref, kj_ref, q_ref, k_ref, v_ref, u_ref, o_ref, acc_ref, carry_ref):
    s = pl.program_id(1)
    qi, kj = qi_ref[s], kj_ref[s]
    diag = kj == qi
    t = q_ref.shape[0]

    @pl.when(diag)
    def _():
        acc_ref[...] = jnp.zeros_like(acc_ref)
        carry_ref[...] = jnp.zeros_like(carry_ref)

    row = lax.broadcasted_iota(jnp.int32, (t, t), 0)
    col = lax.broadcasted_iota(jnp.int32, (t, t), 1)
    keep = jnp.logical_or(jnp.logical_not(diag), col < row)
    for h in range(ATT_HEADS):
        sl = slice(h * ATT_DH, (h + 1) * ATT_DH)
        z = _dot_nt(q_ref[:, sl].astype(BF16), k_ref[:, sl].astype(BF16))
        lk = jnp.where(keep, _log_sigmoid(-z), 0.0)
        sfx = _split_dot_l(lk, u_ref[...], 2)
        c = carry_ref[h]
        w = jnp.where(keep, jnp.exp(z + sfx + c), 0.0)
        acc_ref[:, sl] += _dot(w.astype(BF16), v_ref[:, sl].astype(BF16))
        carry_ref[h] = c + jnp.sum(lk, axis=1, keepdims=True)

    @pl.when(kj == 0)
    def _():
        o_ref[...] = acc_ref[...].astype(o_ref.dtype)


def _fox_prompt_kernel(qi_ref, kj_ref, q_ref, k_ref, v_ref, cq_ref, ck_ref, o_ref, acc_ref, m_ref, l_ref):
    s = pl.program_id(1)
    qi, kj = qi_ref[s], kj_ref[s]
    diag = kj == qi
    t = q_ref.shape[0]

    @pl.when(diag)
    def _():
        acc_ref[...] = jnp.zeros_like(acc_ref)
        l_ref[...] = jnp.zeros_like(l_ref)
        m_ref[...] = jnp.full_like(m_ref, -jnp.inf)

    row = lax.broadcasted_iota(jnp.int32, (t, t), 0)
    col = lax.broadcasted_iota(jnp.int32, (t, t), 1)
    keep = jnp.logical_or(jnp.logical_not(diag), col <= row)
    for h in range(ATT_HEADS):
        sl = slice(h * ATT_DH, (h + 1) * ATT_DH)
        z = _dot_nt(q_ref[:, sl].astype(BF16), k_ref[:, sl].astype(BF16))
        z = z + (cq_ref[:, h:h + 1] - ck_ref[h:h + 1, :])
        z = jnp.where(keep, z, -jnp.inf)
        m_prev = m_ref[h]
        m_new = jnp.maximum(m_prev, jnp.max(z, axis=1, keepdims=True))
        alpha = jnp.exp(m_prev - m_new)
        p = jnp.exp(z - m_new)
        l_ref[h] = alpha * l_ref[h] + jnp.sum(p, axis=1, keepdims=True)
        acc_ref[:, sl] = alpha * acc_ref[:, sl] + _dot(p.astype(BF16), v_ref[:, sl].astype(BF16))
        m_ref[h] = m_new

    @pl.when(kj == 0)
    def _():
        for h in range(ATT_HEADS):
            sl = slice(h * ATT_DH, (h + 1) * ATT_DH)
            o_ref[:, sl] = (acc_ref[:, sl] / l_ref[h]).astype(o_ref.dtype)


def _prompt_attention(kind, q, k, v, n_seq, seq_len, cq=None, ck_t=None):
    t = min(ATT_T, seq_len)
    assert seq_len % t == 0
    nq = seq_len // t
    qi_tab, kj_tab = _tri_tables(nq)
    n_steps = int(qi_tab.shape[0])
    qspec = pl.BlockSpec((t, ATT_W), lambda b, s, qi, kj: (b * nq + qi[s], 0))
    kspec = pl.BlockSpec((t, ATT_W), lambda b, s, qi, kj: (b * nq + kj[s], 0))
    col1 = pltpu.VMEM((ATT_HEADS, t, 1), F32)
    if kind == "sb":
        u = jnp.asarray(np.tril(np.ones((t, t), np.float32)), BF16)
        body, extra, extra_specs = _sb_prompt_kernel, (u,), [pl.BlockSpec((t, t), lambda b, s, qi, kj: (0, 0))]
        scratch = [pltpu.VMEM((t, ATT_W), F32), col1]
    else:
        body, extra = _fox_prompt_kernel, (cq, ck_t)
        extra_specs = [pl.BlockSpec((t, ATT_HEADS), lambda b, s, qi, kj: (b * nq + qi[s], 0)),
                       pl.BlockSpec((None, ATT_HEADS, t), lambda b, s, qi, kj: (b, 0, kj[s]))]
        scratch = [pltpu.VMEM((t, ATT_W), F32), col1, col1]
    return pl.pallas_call(
        body,
        grid_spec=pltpu.PrefetchScalarGridSpec(
            num_scalar_prefetch=2, grid=(n_seq, n_steps),
            in_specs=[qspec, kspec, kspec] + extra_specs,
            out_specs=qspec, scratch_shapes=scratch),
        out_shape=jax.ShapeDtypeStruct(q.shape, BF16),
        compiler_params=_params("arbitrary", "arbitrary"),
        name=kind + "_prompt_attention",
    )(qi_tab, kj_tab, q, k, v, *extra)


def _rep8(x):
    return jnp.concatenate([jnp.broadcast_to(x[h:h + 1, :], (SUBLANE, x.shape[1])) for h in range(ATT_HEADS)],
                           axis=0)


def _fold_heads(a):
    hm = (lax.broadcasted_iota(jnp.int32, a.shape, 1) // ATT_DH) == (lax.broadcasted_iota(jnp.int32, a.shape, 0) // SUBLANE)
    am = jnp.where(hm, a, 0.0)
    out = am[0:SUBLANE]
    for h in range(1, ATT_HEADS):
        out = out + am[h * SUBLANE:(h + 1) * SUBLANE]
    return out


def _decode_kernel(n_p, n_groups, pt_ref, qs_ref, qf_ref, ksn_ref, vsn_ref, kfn_ref, vfn_ref, lfn_ref, *rest):
    ks, vs, kf, vf, lf = (rest[i * n_p:(i + 1) * n_p] for i in range(5))
    usi_ref, use_ref, upi_ref, osb_ref, ofx_ref = rest[5 * n_p:5 * n_p + 5]
    (qbs_ref, qbf_ref, accs_ref, accf_ref, cs_ref, m_ref, l_ref, cd_ref, cq_ref,
     pks_ref, pvs_ref, pkf_ref, pvf_ref) = rest[5 * n_p + 5:]
    del pt_ref
    b, g = pl.program_id(0), pl.program_id(1)
    nq = qs_ref.shape[0]
    rows = ATT_HEADS * nq
    page = LANE
    row = lax.broadcasted_iota(jnp.int32, (rows, page), 0)
    lane = lax.broadcasted_iota(jnp.int32, (rows, page), 1)
    t_of_row = row % nq

    def attend(k_s, v_s, k_f, v_f, bias_f, mask_s, mask_f):
        z = _dot_nt(qbs_ref[...], k_s)
        lk = _log_sigmoid(-z)
        if mask_s is not None:
            lk = jnp.where(mask_s, lk, 0.0)
        sfx = _split_dot_l(lk, usi_ref[...], 2)
        c = cs_ref[...]
        w = jnp.exp(z + sfx + c)
        if mask_s is not None:
            w = jnp.where(mask_s, w, 0.0)
        accs_ref[...] += _dot(w.astype(BF16), v_s)
        cs_ref[...] = c + jnp.sum(lk, axis=1, keepdims=True)

        zf = _dot_nt(qbf_ref[...], k_f) + bias_f
        if mask_f is not None:
            zf = jnp.where(mask_f, zf, -jnp.inf)
        m_prev = m_ref[...]
        m_new = jnp.maximum(m_prev, jnp.max(zf, axis=1, keepdims=True))
        alpha = jnp.exp(m_prev - m_new)
        p = jnp.exp(zf - m_new)
        l_ref[...] = alpha * l_ref[...] + jnp.sum(p, axis=1, keepdims=True)
        accf_ref[...] = alpha * accf_ref[...] + _dot(p.astype(BF16), v_f)
        m_ref[...] = m_new

    @pl.when(jnp.logical_and(b == 0, g == 0))
    def _():
        for r in (pks_ref, pvs_ref, pkf_ref, pvf_ref):
            r[...] = jnp.zeros_like(r)

    @pl.when(g == 0)
    def _():
        shape = (rows, ATT_W)
        hm = (lax.broadcasted_iota(jnp.int32, shape, 1) // ATT_DH) == (lax.broadcasted_iota(jnp.int32, shape, 0) // nq)
        for q_ref, qb_ref in ((qs_ref, qbs_ref), (qf_ref, qbf_ref)):
            qrep = jnp.concatenate([q_ref[...]] * ATT_HEADS, axis=0)
            qb_ref[...] = jnp.where(hm, qrep, 0.0).astype(BF16)
        accs_ref[...] = jnp.zeros_like(accs_ref)
        accf_ref[...] = jnp.zeros_like(accf_ref)
        cs_ref[...] = jnp.zeros_like(cs_ref)
        l_ref[...] = jnp.zeros_like(l_ref)
        m_ref[...] = jnp.full_like(m_ref, -jnp.inf)
        cd_ref[...] = jnp.zeros_like(cd_ref)
        for src, dst in ((ksn_ref, pks_ref), (vsn_ref, pvs_ref), (kfn_ref, pkf_ref), (vfn_ref, pvf_ref)):
            dst[0:nq, :] = src[...]
        c_new = _rep8(_split_dot_l(lfn_ref[...], upi_ref[...], 3))
        cq = jnp.sum(jnp.where(lane == t_of_row, c_new, 0.0), axis=1, keepdims=True)
        cq_ref[...] = cq
        attend(pks_ref[...].astype(BF16), pvs_ref[...].astype(BF16), pkf_ref[...].astype(BF16),
               pvf_ref[...].astype(BF16), cq - c_new, lane < t_of_row, lane <= t_of_row)

    for i in range(n_p):
        lfp = lf[i][...]
        d = _split_dot_l(lfp, use_ref[...], 3) + cd_ref[...]
        cd_ref[...] += jnp.sum(lfp, axis=1, keepdims=True)
        attend(ks[i][...].astype(BF16), vs[i][...].astype(BF16), kf[i][...].astype(BF16),
               vf[i][...].astype(BF16), _rep8(d) + cq_ref[...], None, None)

    @pl.when(g == n_groups - 1)
    def _():
        osb_ref[...] = _fold_heads(accs_ref[...])
        ofx_ref[...] = _fold_heads(accf_ref[...] / l_ref[...])


def _decode_attention(layer, page_table, qs, qf, ksn, vsn, kfn, vfn, lfn_t, c_sb_k, c_sb_v, c_fx_k, c_fx_v,
                      c_lf_t):
    n_seq, n_pages = page_table.shape
    nq = qs.shape[0] // n_seq
    assert nq == SUBLANE and c_sb_k.shape[2] == LANE
    n_p = min(DEC_NP, n_pages)
    assert n_pages % n_p == 0
    n_groups = n_pages // n_p
    rows = ATT_HEADS * nq
    ones = np.ones((LANE, LANE), np.float32)
    usi = jnp.asarray(np.tril(ones), BF16)
    use = jnp.asarray(np.tril(ones, -1), BF16)
    upi = jnp.asarray(np.triu(ones), BF16)
    new = pl.BlockSpec((nq, ATT_W), lambda b, g, pt: (b, 0))

    def page_spec(shape, i):
        def imap(b, g, pt):
            return (layer, pt[b * n_pages + n_pages - 1 - (g * n_p + i)], 0, 0)
        return pl.BlockSpec((None, None) + shape, imap)

    kv_specs = [page_spec((LANE, ATT_W), i) for i in range(n_p)]
    lf_specs = [page_spec((ATT_HEADS, LANE), i) for i in range(n_p)]
    const = pl.BlockSpec((LANE, LANE), lambda b, g, pt: (0, 0))
    col = pltpu.VMEM((rows, 1), F32)
    pad = pltpu.VMEM((LANE, ATT_W), F32)
    return pl.pallas_call(
        functools.partial(_decode_kernel, n_p, n_groups),
        grid_spec=pltpu.PrefetchScalarGridSpec(
            num_scalar_prefetch=1, grid=(n_seq, n_groups),
            in_specs=[new] * 6 + [pl.BlockSpec((None, ATT_HEADS, LANE), lambda b, g, pt: (b, 0, 0))]
            + kv_specs * 4 + lf_specs + [const] * 3,
            out_specs=[new, new],
            scratch_shapes=[pltpu.VMEM((rows, ATT_W), BF16), pltpu.VMEM((rows, ATT_W), BF16),
                            pltpu.VMEM((rows, ATT_W), F32), pltpu.VMEM((rows, ATT_W), F32),
                            col, col, col, pltpu.VMEM((ATT_HEADS, 1), F32), col, pad, pad, pad, pad]),
        out_shape=[jax.ShapeDtypeStruct(qs.shape, F32)] * 2,
        compiler_params=_params("arbitrary", "arbitrary"),
        name="decode_attention",
    )(page_table.reshape(-1), qs, qf, ksn, vsn, kfn, vfn, lfn_t,
      *([c_sb_k] * n_p), *([c_sb_v] * n_p), *([c_fx_k] * n_p), *([c_fx_v] * n_p), *([c_lf_t] * n_p),
      usi, use, upi)


def _gla_kernel(chunk, per_chunk_state, n_tiles, q_ref, k_ref, v_ref, la_ref, gg_ref, *rest):
    if per_chunk_state:
        s0_ref, rest = rest[0], rest[1:]
    (bdi_ref, bdf_ref, mexp_ref, bmask_ref, gout_ref, o_ref, st_ref,
     kh_ref, bh_ref, vh_ref, qt_ref, kt_ref, eb_ref, vt_ref, oi_ref, ox_ref, sbd_ref) = rest
    tl = q_ref.shape[0]
    n_chunks = tl // chunk
    tile = pl.program_id(1)

    @pl.when(jnp.logical_and(pl.program_id(0) == 0, tile == 0))
    def _():
        for r in (kh_ref, bh_ref, vh_ref):
            r[0:chunk, :] = jnp.zeros((chunk, r.shape[1]), F32)

    @pl.when(tile == 0)
    def _():
        sbd_ref[...] = jnp.zeros_like(sbd_ref)

    q, k, v, la = q_ref[...], k_ref[...], v_ref[...], la_ref[...]
    b = _split_dot_r(bdi_ref[...], la, 3)
    bend = _split_dot_r(bdf_ref[...], la, 3)
    qt_ref[...] = q * jnp.exp(b)
    kt_ref[...] = k * jnp.exp(bend - b)
    eb_ref[...] = jnp.exp(bend)
    vt_ref[...] = v.T.astype(BF16)
    kh_ref[chunk:, :] = k
    bh_ref[chunk:, :] = b
    vh_ref[chunk:, :] = v

    pos = lax.broadcasted_iota(jnp.int32, (tl, GLA_KW), 0) % chunk
    oi_ref[...] = jnp.zeros_like(oi_ref)
    for delta in range(chunk):
        ksh = kh_ref[chunk - delta:chunk - delta + tl, :]
        bsh = bh_ref[chunk - delta:chunk - delta + tl, :]
        vsh = vh_ref[chunk - delta:chunk - delta + tl, :]
        p = jnp.where(pos >= delta, q * ksh * jnp.exp(b - bsh), 0.0)
        a = _split_dot_l(p, mexp_ref[...], 2)
        oi_ref[...] += a * vsh

    rowid = lax.broadcasted_iota(jnp.int32, (tl, GLA_KW), 0)

    def chunk_step(i, carry):
        r = pl.multiple_of(i * chunk, chunk)
        if per_chunk_state:
            for h in range(GLA_HEADS):
                sbd_ref[h * GLA_DV:(h + 1) * GLA_DV, h * GLA_DK:(h + 1) * GLA_DK] = s0_ref[i, h]
        s = sbd_ref[...]
        ox_ref[pl.ds(r, chunk), :] = _dot_nt(qt_ref[pl.ds(r, chunk), :].astype(BF16), s.astype(BF16))
        in_chunk = jnp.logical_and(rowid >= r, rowid < r + chunk)
        km = jnp.where(in_chunk, kt_ref[...], 0.0).astype(BF16)
        ut = _dot(vt_ref[...], km)
        s_new = eb_ref[pl.ds(r, 1), :] * s + ut * bmask_ref[...]
        sbd_ref[...] = s_new
        if per_chunk_state:
            for h in range(GLA_HEADS):
                st_ref[i, h] = s_new[h * GLA_DV:(h + 1) * GLA_DV, h * GLA_DK:(h + 1) * GLA_DK]
        return carry

    lax.fori_loop(0, n_chunks, chunk_step, 0)

    if not per_chunk_state:
        @pl.when(tile == n_tiles - 1)
        def _():
            s = sbd_ref[...]
            for h in range(GLA_HEADS):
                st_ref[0, h] = s[h * GLA_DV:(h + 1) * GLA_DV, h * GLA_DK:(h + 1) * GLA_DK]

    o = oi_ref[...] + ox_ref[...]
    gate = gg_ref[...]
    gate = gate * _sigmoid(gate)
    for h in range(GLA_HEADS):
        sl = slice(h * GLA_DV, (h + 1) * GLA_DV)
        o_ref[:, sl] = (_rmsnorm(o[:, sl], gout_ref[...]) * gate[:, sl]).astype(o_ref.dtype)


def _gla(q, k, v, la, gg, g_out, n_seq, seq_len, s0_t=None):
    t_total = q.shape[0]
    per_chunk_state = s0_t is not None
    chunk = GLA_CHUNK if seq_len % GLA_CHUNK == 0 else seq_len
    tl = min(GLA_TL, t_total)
    assert tl % chunk == 0 and t_total % tl == 0
    if per_chunk_state:
        assert chunk == seq_len
        n_groups, n_tiles, n_st = t_total // tl, 1, tl // chunk
    else:
        assert seq_len % tl == 0
        n_groups, n_tiles, n_st = n_seq, seq_len // tl, 1
    idx = np.arange(tl)
    same = (idx[:, None] // chunk) == (idx[None, :] // chunk)
    bdi = jnp.asarray(same & (idx[None, :] <= idx[:, None]), BF16)
    bdf = jnp.asarray(same, BF16)
    mexp = jnp.asarray((np.arange(GLA_KW)[:, None] // GLA_DK) == (np.arange(GLA_VW)[None, :] // GLA_DV), BF16)
    bmask = jnp.asarray((np.arange(GLA_VW)[:, None] // GLA_DV) == (np.arange(GLA_KW)[None, :] // GLA_DK), F32)
    row = lambda w: pl.BlockSpec((tl, w), lambda s, i: (s * n_tiles + i, 0))
    st_spec = pl.BlockSpec((n_st, GLA_HEADS, GLA_DV, GLA_DK), lambda s, i: (s, 0, 0, 0))
    in_specs = [row(GLA_KW), row(GLA_KW), row(GLA_VW), row(GLA_KW), row(GLA_VW)]
    args = [q, k, v, la, gg]
    if per_chunk_state:
        in_specs.append(st_spec)
        args.append(s0_t)
    in_specs += [_resident((tl, tl)), _resident((tl, tl)), _resident((GLA_KW, GLA_VW)),
                 _resident((GLA_VW, GLA_KW)), _resident((1, GLA_DV))]
    args += [bdi, bdf, mexp, bmask, g_out]
    vm = lambda r, c, dt=F32: pltpu.VMEM((r, c), dt)
    scratch = [vm(tl + chunk, GLA_KW), vm(tl + chunk, GLA_KW), vm(tl + chunk, GLA_VW),
               vm(tl, GLA_KW), vm(tl, GLA_KW), vm(tl, GLA_KW), vm(GLA_VW, tl, BF16),
               vm(tl, GLA_VW), vm(tl, GLA_VW), vm(GLA_VW, GLA_KW)]
    return pl.pallas_call(
        functools.partial(_gla_kernel, chunk, per_chunk_state, n_tiles),
        grid=(n_groups, n_tiles),
        in_specs=in_specs,
        out_specs=[row(GLA_VW), st_spec],
        out_shape=[jax.ShapeDtypeStruct((t_total, GLA_VW), BF16),
                   jax.ShapeDtypeStruct((n_groups * n_st, GLA_HEADS, GLA_DV, GLA_DK), F32)],
        scratch_shapes=scratch,
        compiler_params=_params("arbitrary", "arbitrary"),
        name="gla",
    )(*args)


def _merge_kernel(x_ref, og_ref, os_ref, of_ref, gates_ref, wg_ref, ws_ref, wf_ref, wo_ref, o_ref):
    d = x_ref.shape[1]
    m = None
    for i, (b_ref, w_ref) in enumerate(((og_ref, wg_ref), (os_ref, ws_ref), (of_ref, wf_ref))):
        t = _sigmoid(gates_ref[:, i * d:(i + 1) * d]) * _dot(b_ref[...].astype(BF16), w_ref[...])
        m = t if m is None else m + t
    o_ref[...] = x_ref[...] + _dot(m.astype(BF16), wo_ref[...])


def _merge(x, o_gla, o_sb, o_fx, gates, w_g, w_s, w_f, w_o):
    t, d = x.shape
    tm = min(MERGE_TM, t)
    assert t % tm == 0
    row = lambda w: pl.BlockSpec((tm, w), lambda i: (i, 0))
    return pl.pallas_call(
        _merge_kernel,
        grid=(t // tm,),
        in_specs=[row(d), row(GLA_VW), row(ATT_W), row(ATT_W), row(3 * d), _resident(w_g.shape),
                  _resident(w_s.shape), _resident(w_f.shape), _resident(w_o.shape)],
        out_specs=row(d),
        out_shape=jax.ShapeDtypeStruct(x.shape, F32),
        compiler_params=_params("arbitrary"),
        name="merge",
    )(x, o_gla, o_sb, o_fx, gates, w_g, w_s, w_f, w_o)


def _ffn_kernel(carry_state, final_norm, tiles_per_seq, seq_rows, x_ref, g_ref, win_ref, wc_ref, bc_ref, wd_ref,
                gfin_ref, *rest):
    if carry_state:
        o_ref, conv_ref, act_ref, carry_ref = rest
    else:
        p1_ref, p2_ref, o_ref, u_ref, act_ref = rest
    tm = x_ref.shape[0]
    d_ff = wd_ref.shape[0]
    x = x_ref[...]
    h = _rmsnorm(x, g_ref[...]).astype(BF16)
    if carry_state:
        @pl.when(pl.program_id(0) % tiles_per_seq == 0)
        def _():
            carry_ref[...] = jnp.zeros_like(carry_ref)

    for j in range(d_ff // FFN_FB):
        sl = slice(j * FFN_FB, (j + 1) * FFN_FB)
        u = _dot(h, win_ref[:, sl])
        gt = _dot(h, win_ref[:, d_ff + j * FFN_FB:d_ff + (j + 1) * FFN_FB])
        pos = lax.broadcasted_iota(jnp.int32, u.shape, 0)
        r1 = pltpu.roll(u, 1, 0)
        r2 = pltpu.roll(u, 2, 0)
        if carry_state:
            c0 = carry_ref[0:1, sl]
            c1 = carry_ref[1:2, sl]
            u1 = jnp.where(pos == 0, c1, r1)
            u2 = jnp.where(pos == 0, c0, jnp.where(pos == 1, c1, r2))
            carry_ref[:, sl] = u[tm - (CONV_W - 1):, :]
            conv_ref[:, sl] = u[tm - (CONV_W - 1):, :]
        else:
            pos = pos % seq_rows
            u1 = jnp.where(pos == 0, p1_ref[:, sl], r1)
            u2 = jnp.where(pos < 2, p2_ref[:, sl], r2)
            u_ref[:, sl] = u
        uc = bc_ref[:, sl] + wc_ref[0:1, sl] * u2 + wc_ref[1:2, sl] * u1 + wc_ref[2:3, sl] * u
        act_ref[:, sl] = (uc * _sigmoid(uc) * gt).astype(BF16)
    y = x + _dot(act_ref[...], wd_ref[...])
    if final_norm:
        y = _rmsnorm(y, gfin_ref[...])
    o_ref[...] = y


def _conv_ffn(x, g, w_in, w_conv, b_conv, w_down, g_final, final_norm, n_seq, seq_len, past=None):
    t, d = x.shape
    d_ff = w_down.shape[0]
    assert d_ff % FFN_FB == 0 and CONV_W == 3
    carry_state = past is None
    tm = min(FFN_TM if carry_state else FFN_TM // 2, t)
    assert t % tm == 0
    row = lambda w: pl.BlockSpec((tm, w), lambda i: (i, 0))
    in_specs = [row(d), _resident((1, d)), _resident(w_in.shape), _resident(w_conv.shape), _resident((1, d_ff)),
                _resident(w_down.shape), _resident((1, d))]
    args = [x, g, w_in, w_conv, b_conv, w_down, g_final]
    scratch = [pltpu.VMEM((tm, d_ff), BF16)]
    if carry_state:
        assert seq_len % tm == 0
        tiles_per_seq = seq_len // tm
        out_specs = [row(d), pl.BlockSpec((None, CONV_W - 1, d_ff), lambda i: (i // tiles_per_seq, 0, 0))]
        out_shape = [jax.ShapeDtypeStruct(x.shape, F32), jax.ShapeDtypeStruct((n_seq, CONV_W - 1, d_ff), F32)]
        scratch.append(pltpu.VMEM((CONV_W - 1, d_ff), F32))
    else:
        assert seq_len == SUBLANE
        tiles_per_seq = 0
        pad = jnp.zeros((n_seq, seq_len, d_ff), F32)
        p1 = pad.at[:, 0].set(past[:, 1]).reshape(t, d_ff)
        p2 = pad.at[:, 0:2].set(past).reshape(t, d_ff)
        in_specs += [row(d_ff), row(d_ff)]
        args += [p1, p2]
        out_specs = [row(d), row(d_ff)]
        out_shape = [jax.ShapeDtypeStruct(x.shape, F32), jax.ShapeDtypeStruct((t, d_ff), F32)]
    return pl.pallas_call(
        functools.partial(_ffn_kernel, carry_state, final_norm, tiles_per_seq, seq_len),
        grid=(t // tm,),
        in_specs=in_specs,
        out_specs=out_specs,
        out_shape=out_shape,
        scratch_shapes=scratch,
        compiler_params=_params("arbitrary"),
        name="conv_ffn",
    )(*args)


def _pack_w_in(w):
    d = w.shape[0]
    sizes = _in_sizes(d)
    offs, o = {}, 0
    for n in IN_NAMES:
        offs[n] = (o, o + sizes[n])
        o += sizes[n]
    cols = [w[:, offs[n][0]:offs[n][1]] for n in PACK_ORDER]
    tail = jnp.zeros((d, LANE), w.dtype)
    tail = tail.at[:, TAIL_LR:TAIL_LR + GLA_LOWRANK].set(w[:, offs["glr"][0]:offs["glr"][1]])
    tail = tail.at[:, TAIL_FF:TAIL_FF + ATT_HEADS].set(w[:, offs["ff"][0]:offs["ff"][1]])
    return jnp.concatenate(cols + [tail], axis=1).astype(BF16)


def kernel(x_prompt, x_sample, state_gla, cache_sb_k, cache_sb_v, cache_fox_k, cache_fox_v, cache_fox_logf,
           state_ffn_conv, page_table, g_mix, w_in, w_gla_lr, b_gla_lr, g_gla_out, b_fox_f, w_br_gla, w_br_sb,
           w_br_fox, w_o, g_ffn, w_ffn_in, w_conv, b_conv, w_down, g_final):
    bp, lp, d = x_prompt.shape
    bs, ls, _ = x_sample.shape
    depth = w_in.shape[0]
    d_ff = w_down.shape[1]
    n_pool, page = cache_sb_k.shape[1], cache_sb_k.shape[2]
    tp, ts = bp * lp, bs * ls
    xp = x_prompt.reshape(tp, d)
    xs = x_sample.reshape(ts, d)
    c_sb_k = cache_sb_k.reshape(depth, n_pool, page, ATT_W)
    c_sb_v = cache_sb_v.reshape(depth, n_pool, page, ATT_W)
    c_fx_k = cache_fox_k.reshape(depth, n_pool, page, ATT_W)
    c_fx_v = cache_fox_v.reshape(depth, n_pool, page, ATT_W)
    c_lf_t = jnp.swapaxes(cache_fox_logf, 2, 3)
    state_t = jnp.swapaxes(state_gla, 3, 4)
    g_fin = g_final.reshape(1, d)

    outs_p, outs_s = [], []
    for l in range(depth):
        w_packed = _pack_w_in(w_in[l])
        wlr = jnp.zeros((LANE, GLA_KW), F32).at[TAIL_LR:TAIL_LR + GLA_LOWRANK].set(w_gla_lr[l])
        wlr_hi = wlr.astype(BF16)
        wlr_lo = (wlr - wlr_hi.astype(F32)).astype(BF16)
        blr = b_gla_lr[l].reshape(1, GLA_KW)
        bff = jnp.zeros((1, LANE), F32).at[0, TAIL_FF:TAIL_FF + ATT_HEADS].set(b_fox_f[l])
        g_mix_l = g_mix[l].reshape(1, d)
        g_out = g_gla_out[l].reshape(1, GLA_DV)
        w_g, w_s, w_f, w_o_l = (w.astype(BF16) for w in (w_br_gla[l], w_br_sb[l], w_br_fox[l], w_o[l]))
        ffn_w = (g_ffn[l].reshape(1, d), w_ffn_in[l].astype(BF16), w_conv[l], b_conv[l].reshape(1, d_ff),
                 w_down[l].astype(BF16), g_fin)
        final = l == depth - 1

        pp = _project(xp, g_mix_l, w_packed, wlr_hi, wlr_lo, blr, bff)
        logf_p = pp["tail"][:, TAIL_FF:TAIL_FF + ATT_HEADS]
        c_p = _seq_cumsum(pp["tail"], bp, lp)[:, TAIL_FF:TAIL_FF + ATT_HEADS]
        ck_t = jnp.swapaxes(c_p.reshape(bp, lp, ATT_HEADS), 1, 2)
        o_sb = _prompt_attention("sb", pp["sq"], pp["sk"], pp["sv"], bp, lp)
        o_fx = _prompt_attention("fox", pp["fq"], pp["fk"], pp["fv"], bp, lp, c_p, ck_t)
        o_gla, st_p = _gla(pp["gq"], pp["gk"], pp["gv"], pp["la"], pp["gg"], g_out, bp, lp)
        xp = _merge(xp, o_gla, o_sb, o_fx, pp["gates"], w_g, w_s, w_f, w_o_l)
        xp, conv_p = _conv_ffn(xp, *ffn_w, final, bp, lp)
        outs_p.append((jnp.swapaxes(st_p, 2, 3), pp["sk"], pp["sv"], pp["fk"], pp["fv"], logf_p, conv_p))

        ps = _project(xs, g_mix_l, w_packed, wlr_hi, wlr_lo, blr, bff)
        logf_s = ps["tail"][:, TAIL_FF:TAIL_FF + ATT_HEADS]
        lfn_t = jnp.swapaxes(logf_s.reshape(bs, ls, ATT_HEADS), 1, 2)
        lfn_t = jnp.pad(lfn_t, ((0, 0), (0, 0), (0, LANE - ls)))
        o_sb, o_fx = _decode_attention(l, page_table, ps["sq"], ps["fq"], ps["sk"], ps["sv"], ps["fk"], ps["fv"],
                                       lfn_t, c_sb_k, c_sb_v, c_fx_k, c_fx_v, c_lf_t)
        o_gla, st_s = _gla(ps["gq"], ps["gk"], ps["gv"], ps["la"], ps["gg"], g_out, bs, ls, state_t[l])
        xs = _merge(xs, o_gla, o_sb, o_fx, ps["gates"], w_g, w_s, w_f, w_o_l)
        xs, u_s = _conv_ffn(xs, *ffn_w, final, bs, ls, state_ffn_conv[l])
        conv_s = u_s.reshape(bs, ls, d_ff)[:, ls - (CONV_W - 1):]
        outs_s.append((jnp.swapaxes(st_s, 2, 3), ps["sk"], ps["sv"], ps["fk"], ps["fv"], logf_s, conv_s))

    def stack(outs, i, shape):
        return jnp.stack([o[i].reshape(shape) for o in outs])

    kv_p, kv_s = (bp, lp, ATT_HEADS, ATT_DH), (bs, ls, ATT_HEADS, ATT_DH)
    st_shape_p, st_shape_s = (bp, GLA_HEADS, GLA_DK, GLA_DV), (bs, GLA_HEADS, GLA_DK, GLA_DV)
    return (xp.reshape(bp, lp, d), xs.reshape(bs, ls, d),
            stack(outs_p, 0, st_shape_p), stack(outs_s, 0, st_shape_s),
            stack(outs_p, 1, kv_p), stack(outs_p, 2, kv_p), stack(outs_s, 1, kv_s), stack(outs_s, 2, kv_s),
            stack(outs_p, 3, kv_p), stack(outs_p, 4, kv_p), stack(outs_p, 5, (bp, lp, ATT_HEADS)),
            stack(outs_s, 3, kv_s), stack(outs_s, 4, kv_s), stack(outs_s, 5, (bs, ls, ATT_HEADS)),
            stack(outs_p, 6, (bp, CONV_W - 1, d_ff)), stack(outs_s, 6, (bs, CONV_W - 1, d_ff)))
```

```python
import functools

import numpy as np
import jax
import jax.numpy as jnp
from jax import lax
from jax.experimental import pallas as pl
from jax.experimental.pallas import tpu as pltpu

F32, BF16 = jnp.float32, jnp.bfloat16

GLA_HEADS, GLA_DK, GLA_DV, GLA_LOWRANK, GLA_TAU, GLA_CHUNK = 4, 64, 128, 16, 16.0, 16
ATT_HEADS, ATT_DH = 8, 64
NORM_EPS = 1e-6
CONV_W = 3
GLA_KW = GLA_HEADS * GLA_DK
GLA_VW = GLA_HEADS * GLA_DV
ATT_W = ATT_HEADS * ATT_DH
IN_NAMES = ("gq", "gk", "gv", "glr", "gg", "sq", "sk", "sv", "fq", "fk", "fv", "ff", "gates")

LANE = 128
SUBLANE = 8
VMEM_LIMIT_BYTES = 52 * 1024 * 1024

PACK_ORDER = ("gq", "gk", "gv", "gg", "sq", "sk", "sv", "fq", "fk", "fv", "gates")
TAIL_LR = 0
TAIL_FF = GLA_LOWRANK

PROJ_TM = 256
ATT_T = 256
GLA_TL = 256
MERGE_TM = 512
FFN_TM = 512
FFN_FB = 256
CUMSUM_TB = 512
DEC_NP = 4


def _in_sizes(d_model):
    return dict(gq=GLA_KW, gk=GLA_KW, gv=GLA_VW, glr=GLA_LOWRANK, gg=GLA_VW, sq=ATT_W, sk=ATT_W, sv=ATT_W,
                fq=ATT_W, fk=ATT_W, fv=ATT_W, ff=ATT_HEADS, gates=3 * d_model)


def _pack_layout(d_model):
    sizes = _in_sizes(d_model)
    off, lay = 0, {}
    for n in PACK_ORDER:
        lay[n] = (off, off + sizes[n])
        off += sizes[n]
    lay["tail"] = (off, off + LANE)
    return lay, off + LANE


def _dot(a, b):
    return jnp.dot(a, b, preferred_element_type=F32)


def _dot_nt(a, b):
    return lax.dot_general(a, b, (((1,), (1,)), ((), ())), preferred_element_type=F32)


def _split(a, n):
    parts, r = [], a
    for i in range(n):
        p = r.astype(BF16)
        parts.append(p)
        if i + 1 < n:
            r = r - p.astype(F32)
    return parts


def _split_dot_l(a, b_bf, n):
    out = None
    for p in _split(a, n):
        t = _dot(p, b_bf)
        out = t if out is None else out + t
    return out


def _split_dot_r(a_bf, b, n):
    out = None
    for p in _split(b, n):
        t = _dot(a_bf, p)
        out = t if out is None else out + t
    return out


def _log_sigmoid(x):
    return jnp.minimum(x, 0.0) - jnp.log1p(jnp.exp(-jnp.abs(x)))


def _neg_softplus(x):
    return -(jnp.maximum(x, 0.0) + jnp.log(1.0 + jnp.exp(-jnp.abs(x))))


def _sigmoid(x):
    return 1.0 / (1.0 + jnp.exp(-x))


def _rmsnorm(x, g):
    return x * lax.rsqrt(jnp.mean(x * x, axis=-1, keepdims=True) + NORM_EPS) * g


def _params(*sem):
    return pltpu.CompilerParams(dimension_semantics=sem, vmem_limit_bytes=VMEM_LIMIT_BYTES)


def _resident(shape):
    nd = len(shape)
    return pl.BlockSpec(shape, lambda *_: (0,) * nd, pipeline_mode=pl.Buffered(1))


PROJ_EXTRA = ("sqm", "fqm", "skb", "fkb", "svt", "fvt")


def _proj_kernel(lay, x_ref, g_ref, w_ref, wvt_ref, wlr_hi_ref, wlr_lo_ref, blr_ref, bff_ref, *out_refs):
    outs = dict(zip(PACK_ORDER + ("la", "tail") + PROJ_EXTRA, out_refs))
    h = _rmsnorm(x_ref[...], g_ref[...]).astype(BF16)

    def seg(name):
        a, b = lay[name]
        return _dot(h, w_ref[:, a:b])

    scale = ATT_DH ** -0.5
    for n in PACK_ORDER:
        z = seg(n)
        if n in ("gq", "sq", "fq"):
            z = z * scale
        outs[n][...] = z
        if n in ("sq", "fq"):
            low = lax.broadcasted_iota(jnp.int32, (z.shape[0], LANE), 1) < ATT_DH
            for hd in range(ATT_HEADS):
                pair = z[:, (hd // 2) * LANE:(hd // 2 + 1) * LANE]
                own = low if hd % 2 == 0 else jnp.logical_not(low)
                outs[n + "m"][:, hd * LANE:(hd + 1) * LANE] = jnp.where(own, pair, 0.0).astype(BF16)
        if n in ("sk", "fk"):
            outs[n + "b"][...] = z.astype(BF16)
    vt = _dot_nt(wvt_ref[...], h)
    outs["svt"][...] = vt[:ATT_W].astype(BF16)
    outs["fvt"][...] = vt[ATT_W:].astype(BF16)
    tail = seg("tail")
    t_hi, t_lo = _split(tail, 2)
    xlr = _dot(t_hi, wlr_hi_ref[...]) + _dot(t_lo, wlr_hi_ref[...]) + _dot(t_hi, wlr_lo_ref[...])
    outs["la"][...] = _log_sigmoid(xlr + blr_ref[...]) * (1.0 / GLA_TAU)
    outs["tail"][...] = _log_sigmoid(tail + bff_ref[...])


def _project(x, g, w_packed, w_vt, wlr_hi, wlr_lo, blr, bff):
    t, d = x.shape
    lay, width = _pack_layout(d)
    tm = min(PROJ_TM, t)
    assert t % tm == 0
    names = PACK_ORDER + ("la", "tail")
    widths = {n: lay[n][1] - lay[n][0] for n in PACK_ORDER}
    widths["la"] = GLA_KW
    widths["tail"] = LANE
    row = lambda w: pl.BlockSpec((tm, w), lambda i: (i, 0))
    col = pl.BlockSpec((ATT_W, tm), lambda i: (0, i))
    extra_specs = [row(ATT_HEADS * LANE), row(ATT_HEADS * LANE), row(ATT_W), row(ATT_W), col, col]
    extra_shapes = [(t, ATT_HEADS * LANE), (t, ATT_HEADS * LANE), (t, ATT_W), (t, ATT_W), (ATT_W, t), (ATT_W, t)]
    outs = pl.pallas_call(
        functools.partial(_proj_kernel, lay),
        grid=(t // tm,),
        in_specs=[row(d), _resident((1, d)), _resident((d, width)), _resident((2 * ATT_W, d)),
                  _resident((LANE, GLA_KW)), _resident((LANE, GLA_KW)), _resident((1, GLA_KW)), _resident((1, LANE))],
        out_specs=[row(widths[n]) for n in names] + extra_specs,
        out_shape=[jax.ShapeDtypeStruct((t, widths[n]), F32) for n in names]
        + [jax.ShapeDtypeStruct(s, BF16) for s in extra_shapes],
        compiler_params=_params("arbitrary"),
        name="project",
    )(x, g, w_packed, w_vt, wlr_hi, wlr_lo, blr, bff)
    return dict(zip(names + PROJ_EXTRA, outs))


C_PIECES = 3
C_ONES = C_PIECES * ATT_HEADS


def _cumsum_kernel(x_ref, tri_ref, sk_ref, sq_ref, qconst_ref, ckp_ref, cqa_ref, carry_ref):
    @pl.when(pl.program_id(1) == 0)
    def _():
        carry_ref[...] = jnp.zeros_like(carry_ref)

    c = _split_dot_r(tri_ref[...], x_ref[...], 3) + carry_ref[...]
    carry_ref[...] = c[-1:, :]
    ck, cq = None, None
    for j, p in enumerate(_split(c, C_PIECES)):
        tk, tq = _dot(p, sk_ref[j]), _dot(p, sq_ref[j])
        ck = tk if ck is None else ck + tk
        cq = tq if cq is None else cq + tq
    lane = lax.broadcasted_iota(jnp.int32, ck.shape, 1)
    ones = jnp.logical_and(lane >= C_ONES, lane < C_ONES + C_PIECES)
    ckp_ref[...] = jnp.where(ones, 1.0, -ck).astype(BF16)
    cqa_ref[...] = (cq + qconst_ref[...]).astype(BF16)


def _seq_cumsum(x, n_seq, seq_len):
    tb = min(CUMSUM_TB, seq_len)
    assert seq_len % tb == 0
    nt = seq_len // tb
    tri = jnp.asarray(np.tril(np.ones((tb, tb), np.float32)), BF16)
    sel_k = np.zeros((C_PIECES, LANE, LANE), np.float32)
    sel_q = np.zeros((C_PIECES, LANE, ATT_HEADS * LANE), np.float32)
    qconst = np.zeros((1, ATT_HEADS * LANE), np.float32)
    for h in range(ATT_HEADS):
        for j in range(C_PIECES):
            sel_k[j, TAIL_FF + h, C_PIECES * h + j] = 1.0
            sel_q[j, TAIL_FF + h, h * LANE + C_ONES + j] = 1.0
            qconst[0, h * LANE + C_PIECES * h + j] = 1.0
    row = lambda w: pl.BlockSpec((tb, w), lambda b, i: (b * nt + i, 0))
    return pl.pallas_call(
        _cumsum_kernel,
        grid=(n_seq, nt),
        in_specs=[row(LANE), _resident((tb, tb)), _resident(sel_k.shape), _resident(sel_q.shape),
                  _resident(qconst.shape)],
        out_specs=[row(LANE), row(ATT_HEADS * LANE)],
        out_shape=[jax.ShapeDtypeStruct((x.shape[0], LANE), BF16),
                   jax.ShapeDtypeStruct((x.shape[0], ATT_HEADS * LANE), BF16)],
        scratch_shapes=[pltpu.VMEM((1, LANE), F32)],
        compiler_params=_params("arbitrary", "arbitrary"),
        name="logf_cumsum",
    )(x, tri, jnp.asarray(sel_k, BF16), jnp.asarray(sel_q, BF16), jnp.asarray(qconst))


def _tri_tables(nq):
    qi = [q for q in range(nq) for _ in range(q + 1)]
    kj = [k for q in range(nq) for k in range(q, -1, -1)]
    return jnp.asarray(qi, jnp.int32), jnp.asarray(kj, jnp.int32)


def _sb_prompt_kernel(qi_ref, kj_ref, qm_ref, k_ref, vt_ref, u_ref, o_ref, acc_ref, carry_ref):
    s = pl.program_id(1)
    qi, kj = qi_ref[s], kj_ref[s]
    tk, tq = k_ref.shape[0], qm_ref.shape[0]

    @pl.when(kj == qi)
    def _():
        acc_ref[...] = jnp.zeros_like(acc_ref)
        carry_ref[...] = jnp.zeros_like(carry_ref)

    def step(masked):
        if masked:
            keep = lax.broadcasted_iota(jnp.int32, (tk, tq), 0) < lax.broadcasted_iota(jnp.int32, (tk, tq), 1)
        carry = carry_ref[...]
        heads = range(ATT_HEADS)
        zts = [_dot_nt(k_ref[:, (h // 2) * LANE:(h // 2 + 1) * LANE], qm_ref[:, h * LANE:(h + 1) * LANE])
               for h in heads]
        lks = [_neg_softplus(zt) for zt in zts]
        if masked:
            lks = [jnp.where(keep, lk, 0.0) for lk in lks]
        sfxs = [_split_dot_r(u_ref[...], lk, 2) for lk in lks]
        ws = [jnp.exp(zts[h] + sfxs[h] + carry[h:h + 1, :]) for h in heads]
        if masked:
            ws = [jnp.where(keep, w, 0.0) for w in ws]
        for h in heads:
            rows = slice(h * ATT_DH, (h + 1) * ATT_DH)
            acc_ref[rows, :] += _dot(vt_ref[rows, :], ws[h].astype(BF16))
        carry_ref[...] = carry + jnp.concatenate([sfx[0:1, :] for sfx in sfxs], axis=0)

    pl.when(kj == qi)(functools.partial(step, True))
    pl.when(kj != qi)(functools.partial(step, False))

    @pl.when(kj == 0)
    def _():
        o_ref[...] = acc_ref[...].T.astype(o_ref.dtype)


def _fox_prompt_kernel(qi_ref, kj_ref, qm_ref, cqa_ref, k_ref, ckp_ref, vt_ref, o_ref, acc_ref, m_ref, l_ref):
    s = pl.program_id(1)
    qi, kj = qi_ref[s], kj_ref[s]
    tk, tq = k_ref.shape[0], qm_ref.shape[0]

    @pl.when(kj == qi)
    def _():
        acc_ref[...] = jnp.zeros_like(acc_ref)
        l_ref[...] = jnp.zeros_like(l_ref)
        m_ref[...] = jnp.full_like(m_ref, -jnp.inf)

    def step(masked):
        if masked:
            keep = lax.broadcasted_iota(jnp.int32, (tk, tq), 0) <= lax.broadcasted_iota(jnp.int32, (tk, tq), 1)
        m_prev, l_prev = m_ref[...], l_ref[...]
        heads = range(ATT_HEADS)
        zts = []
        for h in heads:
            pair = slice((h // 2) * LANE, (h // 2 + 1) * LANE)
            blk = slice(h * LANE, (h + 1) * LANE)
            ka = jnp.concatenate([k_ref[:, pair], ckp_ref[...]], axis=1)
            qa = jnp.concatenate([qm_ref[:, blk], cqa_ref[:, blk]], axis=1)
            zts.append(_dot_nt(ka, qa))
        if masked:
            zts = [jnp.where(keep, zt, -jnp.inf) for zt in zts]
        m_new = jnp.maximum(m_prev, jnp.concatenate([jnp.max(zt, axis=0, keepdims=True) for zt in zts], axis=0))
        alpha = jnp.exp(m_prev - m_new)
        ps = [jnp.exp(zts[h] - m_new[h:h + 1, :]) for h in heads]
        l_ref[...] = alpha * l_prev + jnp.concatenate([jnp.sum(p, axis=0, keepdims=True) for p in ps], axis=0)
        m_ref[...] = m_new
        for h in heads:
            rows = slice(h * ATT_DH, (h + 1) * ATT_DH)
            acc_ref[rows, :] = alpha[h:h + 1, :] * acc_ref[rows, :] + _dot(vt_ref[rows, :], ps[h].astype(BF16))

    pl.when(kj == qi)(functools.partial(step, True))
    pl.when(kj != qi)(functools.partial(step, False))

    @pl.when(kj == 0)
    def _():
        l_all = l_ref[...]
        inv = jnp.concatenate([jnp.broadcast_to(1.0 / l_all[h:h + 1, :], (ATT_DH, tq)) for h in range(ATT_HEADS)],
                              axis=0)
        o_ref[...] = (acc_ref[...] * inv).T.astype(o_ref.dtype)


def _prompt_attention(kind, qm, k, vt, n_seq, seq_len, cqa=None, ckp=None):
    t = min(ATT_T, seq_len)
    assert seq_len % t == 0
    nq = seq_len // t
    qi_tab, kj_tab = _tri_tables(nq)
    n_steps = int(qi_tab.shape[0])
    qrow = lambda w: pl.BlockSpec((t, w), lambda b, s, qi, kj: (b * nq + qi[s], 0))
    krow = lambda w: pl.BlockSpec((t, w), lambda b, s, qi, kj: (b * nq + kj[s], 0))
    vt_spec = pl.BlockSpec((ATT_W, t), lambda b, s, qi, kj: (0, b * nq + kj[s]))
    stat = pltpu.VMEM((ATT_HEADS, t), F32)
    acc = pltpu.VMEM((ATT_W, t), F32)
    if kind == "sb":
        u = jnp.asarray(np.triu(np.ones((t, t), np.float32)), BF16)
        body, args = _sb_prompt_kernel, (qm, k, vt, u)
        in_specs = [qrow(ATT_HEADS * LANE), krow(ATT_W), vt_spec, pl.BlockSpec((t, t), lambda b, s, qi, kj: (0, 0))]
        scratch = [acc, stat]
    else:
        body, args = _fox_prompt_kernel, (qm, cqa, k, ckp, vt)
        in_specs = [qrow(ATT_HEADS * LANE), qrow(ATT_HEADS * LANE), krow(ATT_W), krow(LANE), vt_spec]
        scratch = [acc, stat, stat]
    return pl.pallas_call(
        body,
        grid_spec=pltpu.PrefetchScalarGridSpec(
            num_scalar_prefetch=2, grid=(n_seq, n_steps),
            in_specs=in_specs, out_specs=qrow(ATT_W), scratch_shapes=scratch),
        out_shape=jax.ShapeDtypeStruct((n_seq * seq_len, ATT_W), BF16),
        compiler_params=_params("arbitrary", "arbitrary"),
        name=kind + "_prompt_attention",
    )(qi_tab, kj_tab, *args)


def _rep8(x):
    return jnp.concatenate([jnp.broadcast_to(x[h:h + 1, :], (SUBLANE, x.shape[1])) for h in range(ATT_HEADS)],
                           axis=0)


def _fold_heads(a):
    hm = (lax.broadcasted_iota(jnp.int32, a.shape, 1) // ATT_DH) == (lax.broadcasted_iota(jnp.int32, a.shape, 0) // SUBLANE)
    am = jnp.where(hm, a, 0.0)
    out = am[0:SUBLANE]
    for h in range(1, ATT_HEADS):
        out = out + am[h * SUBLANE:(h + 1) * SUBLANE]
    return out


def _page_rows(page_refs):
    pages = [jnp.concatenate([r[pl.ds(h, LANE, stride=ATT_HEADS), :] for h in range(ATT_HEADS)], axis=1)
             for r in page_refs]
    return jnp.concatenate(pages, axis=0).astype(BF16)


def _decode_kernel(n_p, n_groups, pt_ref, qs_ref, qf_ref, ksn_ref, vsn_ref, kfn_ref, vfn_ref, lfn_ref, *rest):
    ks, vs, kf, vf, lf = (rest[i * n_p:(i + 1) * n_p] for i in range(5))
    usi_ref, use_ref, upi_ref, osb_ref, ofx_ref = rest[5 * n_p:5 * n_p + 5]
    (qbs_ref, qbf_ref, accs_ref, accf_ref, cs_ref, m_ref, l_ref, cd_ref, cq_ref,
     pks_ref, pvs_ref, pkf_ref, pvf_ref) = rest[5 * n_p + 5:]
    del pt_ref
    b, g = pl.program_id(0), pl.program_id(1)
    nq = qs_ref.shape[0]
    rows = ATT_HEADS * nq
    page = LANE
    row = lax.broadcasted_iota(jnp.int32, (rows, page), 0)
    lane = lax.broadcasted_iota(jnp.int32, (rows, page), 1)
    t_of_row = row % nq

    def attend(k_s, v_s, k_f, v_f, bias_f, mask_s, mask_f):
        z = _dot_nt(qbs_ref[...], k_s)
        lk = _log_sigmoid(-z)
        if mask_s is not None:
            lk = jnp.where(mask_s, lk, 0.0)
        usi = usi_ref[...] if mask_s is None else usi_ref[0:page, 0:page]
        sfx = _split_dot_l(lk, usi, 2)
        c = cs_ref[...]
        w = jnp.exp(z + sfx + c)
        if mask_s is not None:
            w = jnp.where(mask_s, w, 0.0)
        accs_ref[...] += _dot(w.astype(BF16), v_s)
        cs_ref[...] = c + jnp.sum(lk, axis=1, keepdims=True)

        zf = _dot_nt(qbf_ref[...], k_f) + bias_f
        if mask_f is not None:
            zf = jnp.where(mask_f, zf, -jnp.inf)
        m_prev = m_ref[...]
        m_new = jnp.maximum(m_prev, jnp.max(zf, axis=1, keepdims=True))
        alpha = jnp.exp(m_prev - m_new)
        p = jnp.exp(zf - m_new)
        l_ref[...] = alpha * l_ref[...] + jnp.sum(p, axis=1, keepdims=True)
        accf_ref[...] = alpha * accf_ref[...] + _dot(p.astype(BF16), v_f)
        m_ref[...] = m_new

    @pl.when(jnp.logical_and(b == 0, g == 0))
    def _():
        for r in (pks_ref, pvs_ref, pkf_ref, pvf_ref):
            r[...] = jnp.zeros_like(r)

    @pl.when(g == 0)
    def _():
        shape = (rows, ATT_W)
        hm = (lax.broadcasted_iota(jnp.int32, shape, 1) // ATT_DH) == (lax.broadcasted_iota(jnp.int32, shape, 0) // nq)
        for q_ref, qb_ref in ((qs_ref, qbs_ref), (qf_ref, qbf_ref)):
            qrep = jnp.concatenate([q_ref[...]] * ATT_HEADS, axis=0)
            qb_ref[...] = jnp.where(hm, qrep, 0.0).astype(BF16)
        accs_ref[...] = jnp.zeros_like(accs_ref)
        accf_ref[...] = jnp.zeros_like(accf_ref)
        cs_ref[...] = jnp.zeros_like(cs_ref)
        l_ref[...] = jnp.zeros_like(l_ref)
        m_ref[...] = jnp.full_like(m_ref, -jnp.inf)
        cd_ref[...] = jnp.zeros_like(cd_ref)
        for src, dst in ((ksn_ref, pks_ref), (vsn_ref, pvs_ref), (kfn_ref, pkf_ref), (vfn_ref, pvf_ref)):
            dst[0:nq, :] = src[...]
        c_new = _rep8(_split_dot_l(lfn_ref[...], upi_ref[...], 3))
        cq = jnp.sum(jnp.where(lane == t_of_row, c_new, 0.0), axis=1, keepdims=True)
        cq_ref[...] = cq
        attend(pks_ref[...].astype(BF16), pvs_ref[...].astype(BF16), pkf_ref[...].astype(BF16),
               pvf_ref[...].astype(BF16), cq - c_new, lane < t_of_row, lane <= t_of_row)

    lfp = jnp.concatenate([r[...] for r in lf], axis=1)
    d = _split_dot_l(lfp, use_ref[...], 3) + cd_ref[...]
    cd_ref[...] += jnp.sum(lfp, axis=1, keepdims=True)
    attend(_page_rows(ks), _page_rows(vs), _page_rows(kf), _page_rows(vf), _rep8(d) + cq_ref[...], None, None)

    @pl.when(g == n_groups - 1)
    def _():
        osb_ref[...] = _fold_heads(accs_ref[...])
        ofx_ref[...] = _fold_heads(accf_ref[...] / l_ref[...])


def _decode_attention(layer, page_table, qs, qf, ksn, vsn, kfn, vfn, lfn_t, c_sb_k, c_sb_v, c_fx_k, c_fx_v,
                      c_lf_t):
    n_seq, n_pages = page_table.shape
    nq = qs.shape[0] // n_seq
    assert nq == SUBLANE and c_sb_k.shape[2:] == (LANE * ATT_HEADS, ATT_DH)
    n_p = min(DEC_NP, n_pages)
    assert n_pages % n_p == 0
    n_groups = n_pages // n_p
    rows = ATT_HEADS * nq
    keys = n_p * LANE
    ones = np.ones((keys, keys), np.float32)
    usi = jnp.asarray(np.tril(ones), BF16)
    use = jnp.asarray(np.tril(ones, -1), BF16)
    upi = jnp.asarray(np.triu(ones[:LANE, :LANE]), BF16)
    new = pl.BlockSpec((nq, ATT_W), lambda b, g, pt: (b, 0))

    def page_spec(shape, i):
        def imap(b, g, pt):
            return (layer, pt[b * n_pages + n_pages - (g + 1) * n_p + i], 0, 0)
        return pl.BlockSpec((None, None) + shape, imap)

    kv_specs = [page_spec((LANE * ATT_HEADS, ATT_DH), i) for i in range(n_p)]
    lf_specs = [page_spec((ATT_HEADS, LANE), i) for i in range(n_p)]
    const = lambda n: pl.BlockSpec((n, n), lambda b, g, pt: (0, 0))
    col = pltpu.VMEM((rows, 1), F32)
    pad = pltpu.VMEM((LANE, ATT_W), F32)
    return pl.pallas_call(
        functools.partial(_decode_kernel, n_p, n_groups),
        grid_spec=pltpu.PrefetchScalarGridSpec(
            num_scalar_prefetch=1, grid=(n_seq, n_groups),
            in_specs=[new] * 6 + [pl.BlockSpec((None, ATT_HEADS, LANE), lambda b, g, pt: (b, 0, 0))]
            + kv_specs * 4 + lf_specs + [const(keys), const(keys), const(LANE)],
            out_specs=[new, new],
            scratch_shapes=[pltpu.VMEM((rows, ATT_W), BF16), pltpu.VMEM((rows, ATT_W), BF16),
                            pltpu.VMEM((rows, ATT_W), F32), pltpu.VMEM((rows, ATT_W), F32),
                            col, col, col, pltpu.VMEM((ATT_HEADS, 1), F32), col, pad, pad, pad, pad]),
        out_shape=[jax.ShapeDtypeStruct(qs.shape, F32)] * 2,
        compiler_params=_params("arbitrary", "arbitrary"),
        name="decode_attention",
    )(page_table.reshape(-1), qs, qf, ksn, vsn, kfn, vfn, lfn_t,
      *([c_sb_k] * n_p), *([c_sb_v] * n_p), *([c_fx_k] * n_p), *([c_fx_v] * n_p), *([c_lf_t] * n_p),
      usi, use, upi)


def _gla_kernel(chunk, per_chunk_state, n_tiles, q_ref, k_ref, v_ref, la_ref, gg_ref, *rest):
    if per_chunk_state:
        s0_ref, rest = rest[0], rest[1:]
    (bdi_ref, bdf_ref, mexp_ref, bmask_ref, gout_ref, o_ref, st_ref,
     kh_ref, bh_ref, vh_ref, qt_ref, kt_ref, eb_ref, vt_ref, oi_ref, ox_ref, sbd_ref) = rest
    tl = q_ref.shape[0]
    n_chunks = tl // chunk
    tile = pl.program_id(1)

    @pl.when(jnp.logical_and(pl.program_id(0) == 0, tile == 0))
    def _():
        for r in (kh_ref, bh_ref, vh_ref):
            r[0:chunk, :] = jnp.zeros((chunk, r.shape[1]), F32)

    @pl.when(tile == 0)
    def _():
        sbd_ref[...] = jnp.zeros_like(sbd_ref)

    q, k, v, la = q_ref[...], k_ref[...], v_ref[...], la_ref[...]
    b = _split_dot_r(bdi_ref[...], la, 3)
    bend = _split_dot_r(bdf_ref[...], la, 3)
    qt_ref[...] = q * jnp.exp(b)
    kt_ref[...] = k * jnp.exp(bend - b)
    eb_ref[...] = jnp.exp(bend)
    vt_ref[...] = v.T.astype(BF16)
    kh_ref[chunk:, :] = k
    bh_ref[chunk:, :] = b
    vh_ref[chunk:, :] = v

    pos = lax.broadcasted_iota(jnp.int32, (tl, GLA_KW), 0) % chunk
    oi_ref[...] = jnp.zeros_like(oi_ref)
    for delta in range(chunk):
        ksh = kh_ref[chunk - delta:chunk - delta + tl, :]
        bsh = bh_ref[chunk - delta:chunk - delta + tl, :]
        vsh = vh_ref[chunk - delta:chunk - delta + tl, :]
        p = jnp.where(pos >= delta, q * ksh * jnp.exp(b - bsh), 0.0)
        a = _split_dot_l(p, mexp_ref[...], 2)
        oi_ref[...] += a * vsh

    rowid = lax.broadcasted_iota(jnp.int32, (tl, GLA_KW), 0)

    def chunk_step(i, carry):
        r = pl.multiple_of(i * chunk, chunk)
        if per_chunk_state:
            for h in range(GLA_HEADS):
                sbd_ref[h * GLA_DV:(h + 1) * GLA_DV, h * GLA_DK:(h + 1) * GLA_DK] = s0_ref[i, h]
        s = sbd_ref[...]
        ox_ref[pl.ds(r, chunk), :] = _dot_nt(qt_ref[pl.ds(r, chunk), :].astype(BF16), s.astype(BF16))
        in_chunk = jnp.logical_and(rowid >= r, rowid < r + chunk)
        km = jnp.where(in_chunk, kt_ref[...], 0.0).astype(BF16)
        ut = _dot(vt_ref[...], km)
        s_new = eb_ref[pl.ds(r, 1), :] * s + ut * bmask_ref[...]
        sbd_ref[...] = s_new
        if per_chunk_state:
            for h in range(GLA_HEADS):
                st_ref[i, h] = s_new[h * GLA_DV:(h + 1) * GLA_DV, h * GLA_DK:(h + 1) * GLA_DK]
        return carry

    lax.fori_loop(0, n_chunks, chunk_step, 0)

    if not per_chunk_state:
        @pl.when(tile == n_tiles - 1)
        def _():
            s = sbd_ref[...]
            for h in range(GLA_HEADS):
                st_ref[0, h] = s[h * GLA_DV:(h + 1) * GLA_DV, h * GLA_DK:(h + 1) * GLA_DK]

    o = oi_ref[...] + ox_ref[...]
    gate = gg_ref[...]
    gate = gate * _sigmoid(gate)
    for h in range(GLA_HEADS):
        sl = slice(h * GLA_DV, (h + 1) * GLA_DV)
        o_ref[:, sl] = (_rmsnorm(o[:, sl], gout_ref[...]) * gate[:, sl]).astype(o_ref.dtype)


def _gla(q, k, v, la, gg, g_out, n_seq, seq_len, s0_t=None):
    t_total = q.shape[0]
    per_chunk_state = s0_t is not None
    chunk = GLA_CHUNK if seq_len % GLA_CHUNK == 0 else seq_len
    tl = min(GLA_TL, t_total)
    assert tl % chunk == 0 and t_total % tl == 0
    if per_chunk_state:
        assert chunk == seq_len
        n_groups, n_tiles, n_st = t_total // tl, 1, tl // chunk
    else:
        assert seq_len % tl == 0
        n_groups, n_tiles, n_st = n_seq, seq_len // tl, 1
    idx = np.arange(tl)
    same = (idx[:, None] // chunk) == (idx[None, :] // chunk)
    bdi = jnp.asarray(same & (idx[None, :] <= idx[:, None]), BF16)
    bdf = jnp.asarray(same, BF16)
    mexp = jnp.asarray((np.arange(GLA_KW)[:, None] // GLA_DK) == (np.arange(GLA_VW)[None, :] // GLA_DV), BF16)
    bmask = jnp.asarray((np.arange(GLA_VW)[:, None] // GLA_DV) == (np.arange(GLA_KW)[None, :] // GLA_DK), F32)
    row = lambda w: pl.BlockSpec((tl, w), lambda s, i: (s * n_tiles + i, 0))
    st_spec = pl.BlockSpec((n_st, GLA_HEADS, GLA_DV, GLA_DK), lambda s, i: (s, 0, 0, 0))
    in_specs = [row(GLA_KW), row(GLA_KW), row(GLA_VW), row(GLA_KW), row(GLA_VW)]
    args = [q, k, v, la, gg]
    if per_chunk_state:
        in_specs.append(st_spec)
        args.append(s0_t)
    in_specs += [_resident((tl, tl)), _resident((tl, tl)), _resident((GLA_KW, GLA_VW)),
                 _resident((GLA_VW, GLA_KW)), _resident((1, GLA_DV))]
    args += [bdi, bdf, mexp, bmask, g_out]
    vm = lambda r, c, dt=F32: pltpu.VMEM((r, c), dt)
    scratch = [vm(tl + chunk, GLA_KW), vm(tl + chunk, GLA_KW), vm(tl + chunk, GLA_VW),
               vm(tl, GLA_KW), vm(tl, GLA_KW), vm(tl, GLA_KW), vm(GLA_VW, tl, BF16),
               vm(tl, GLA_VW), vm(tl, GLA_VW), vm(GLA_VW, GLA_KW)]
    return pl.pallas_call(
        functools.partial(_gla_kernel, chunk, per_chunk_state, n_tiles),
        grid=(n_groups, n_tiles),
        in_specs=in_specs,
        out_specs=[row(GLA_VW), st_spec],
        out_shape=[jax.ShapeDtypeStruct((t_total, GLA_VW), BF16),
                   jax.ShapeDtypeStruct((n_groups * n_st, GLA_HEADS, GLA_DV, GLA_DK), F32)],
        scratch_shapes=scratch,
        compiler_params=_params("arbitrary", "arbitrary"),
        name="gla",
    )(*args)


def _merge_kernel(x_ref, og_ref, os_ref, of_ref, gates_ref, wg_ref, ws_ref, wf_ref, wo_ref, o_ref):
    d = x_ref.shape[1]
    m = None
    for i, (b_ref, w_ref) in enumerate(((og_ref, wg_ref), (os_ref, ws_ref), (of_ref, wf_ref))):
        t = _sigmoid(gates_ref[:, i * d:(i + 1) * d]) * _dot(b_ref[...].astype(BF16), w_ref[...])
        m = t if m is None else m + t
    o_ref[...] = x_ref[...] + _dot(m.astype(BF16), wo_ref[...])


def _merge(x, o_gla, o_sb, o_fx, gates, w_g, w_s, w_f, w_o):
    t, d = x.shape
    tm = min(MERGE_TM, t)
    assert t % tm == 0
    row = lambda w: pl.BlockSpec((tm, w), lambda i: (i, 0))
    return pl.pallas_call(
        _merge_kernel,
        grid=(t // tm,),
        in_specs=[row(d), row(GLA_VW), row(ATT_W), row(ATT_W), row(3 * d), _resident(w_g.shape),
                  _resident(w_s.shape), _resident(w_f.shape), _resident(w_o.shape)],
        out_specs=row(d),
        out_shape=jax.ShapeDtypeStruct(x.shape, F32),
        compiler_params=_params("arbitrary"),
        name="merge",
    )(x, o_gla, o_sb, o_fx, gates, w_g, w_s, w_f, w_o)


def _ffn_kernel(carry_state, final_norm, tiles_per_seq, seq_rows, x_ref, g_ref, win_ref, wc_ref, bc_ref, wd_ref,
                gfin_ref, *rest):
    if carry_state:
        o_ref, conv_ref, act_ref, carry_ref = rest
    else:
        p1_ref, p2_ref, o_ref, u_ref, act_ref = rest
    tm = x_ref.shape[0]
    d_ff = wd_ref.shape[0]
    x = x_ref[...]
    h = _rmsnorm(x, g_ref[...]).astype(BF16)
    if carry_state:
        @pl.when(pl.program_id(0) % tiles_per_seq == 0)
        def _():
            carry_ref[...] = jnp.zeros_like(carry_ref)

    for j in range(d_ff // FFN_FB):
        sl = slice(j * FFN_FB, (j + 1) * FFN_FB)
        u = _dot(h, win_ref[:, sl])
        gt = _dot(h, win_ref[:, d_ff + j * FFN_FB:d_ff + (j + 1) * FFN_FB])
        pos = lax.broadcasted_iota(jnp.int32, u.shape, 0)
        r1 = pltpu.roll(u, 1, 0)
        r2 = pltpu.roll(u, 2, 0)
        if carry_state:
            c0 = carry_ref[0:1, sl]
            c1 = carry_ref[1:2, sl]
            u1 = jnp.where(pos == 0, c1, r1)
            u2 = jnp.where(pos == 0, c0, jnp.where(pos == 1, c1, r2))
            carry_ref[:, sl] = u[tm - (CONV_W - 1):, :]
            conv_ref[:, sl] = u[tm - (CONV_W - 1):, :]
        else:
            pos = pos % seq_rows
            u1 = jnp.where(pos == 0, p1_ref[:, sl], r1)
            u2 = jnp.where(pos < 2, p2_ref[:, sl], r2)
            u_ref[:, sl] = u
        uc = bc_ref[:, sl] + wc_ref[0:1, sl] * u2 + wc_ref[1:2, sl] * u1 + wc_ref[2:3, sl] * u
        act_ref[:, sl] = (uc * _sigmoid(uc) * gt).astype(BF16)
    y = x + _dot(act_ref[...], wd_ref[...])
    if final_norm:
        y = _rmsnorm(y, gfin_ref[...])
    o_ref[...] = y


def _conv_ffn(x, g, w_in, w_conv, b_conv, w_down, g_final, final_norm, n_seq, seq_len, past=None):
    t, d = x.shape
    d_ff = w_down.shape[0]
    assert d_ff % FFN_FB == 0 and CONV_W == 3
    carry_state = past is None
    tm = min(FFN_TM if carry_state else FFN_TM // 2, t)
    assert t % tm == 0
    row = lambda w: pl.BlockSpec((tm, w), lambda i: (i, 0))
    in_specs = [row(d), _resident((1, d)), _resident(w_in.shape), _resident(w_conv.shape), _resident((1, d_ff)),
                _resident(w_down.shape), _resident((1, d))]
    args = [x, g, w_in, w_conv, b_conv, w_down, g_final]
    scratch = [pltpu.VMEM((tm, d_ff), BF16)]
    if carry_state:
        assert seq_len % tm == 0
        tiles_per_seq = seq_len // tm
        out_specs = [row(d), pl.BlockSpec((None, CONV_W - 1, d_ff), lambda i: (i // tiles_per_seq, 0, 0))]
        out_shape = [jax.ShapeDtypeStruct(x.shape, F32), jax.ShapeDtypeStruct((n_seq, CONV_W - 1, d_ff), F32)]
        scratch.append(pltpu.VMEM((CONV_W - 1, d_ff), F32))
    else:
        assert seq_len == SUBLANE
        tiles_per_seq = 0
        pad = jnp.zeros((n_seq, seq_len, d_ff), F32)
        p1 = pad.at[:, 0].set(past[:, 1]).reshape(t, d_ff)
        p2 = pad.at[:, 0:2].set(past).reshape(t, d_ff)
        in_specs += [row(d_ff), row(d_ff)]
        args += [p1, p2]
        out_specs = [row(d), row(d_ff)]
        out_shape = [jax.ShapeDtypeStruct(x.shape, F32), jax.ShapeDtypeStruct((t, d_ff), F32)]
    return pl.pallas_call(
        functools.partial(_ffn_kernel, carry_state, final_norm, tiles_per_seq, seq_len),
        grid=(t // tm,),
        in_specs=in_specs,
        out_specs=out_specs,
        out_shape=out_shape,
        scratch_shapes=scratch,
        compiler_params=_params("arbitrary"),
        name="conv_ffn",
    )(*args)


def _pack_w_in(w):
    d = w.shape[0]
    sizes = _in_sizes(d)
    offs, o = {}, 0
    for n in IN_NAMES:
        offs[n] = (o, o + sizes[n])
        o += sizes[n]
    seg = lambda n: w[:, offs[n][0]:offs[n][1]]
    tail = jnp.zeros((d, LANE), w.dtype)
    tail = tail.at[:, TAIL_LR:TAIL_LR + GLA_LOWRANK].set(seg("glr"))
    tail = tail.at[:, TAIL_FF:TAIL_FF + ATT_HEADS].set(seg("ff"))
    packed = jnp.concatenate([seg(n) for n in PACK_ORDER] + [tail], axis=1).astype(BF16)
    w_vt = jnp.concatenate([seg("sv"), seg("fv")], axis=1).T.astype(BF16)
    return packed, w_vt


def kernel(x_prompt, x_sample, state_gla, cache_sb_k, cache_sb_v, cache_fox_k, cache_fox_v, cache_fox_logf,
           state_ffn_conv, page_table, g_mix, w_in, w_gla_lr, b_gla_lr, g_gla_out, b_fox_f, w_br_gla, w_br_sb,
           w_br_fox, w_o, g_ffn, w_ffn_in, w_conv, b_conv, w_down, g_final):
    bp, lp, d = x_prompt.shape
    bs, ls, _ = x_sample.shape
    depth = w_in.shape[0]
    d_ff = w_down.shape[1]
    n_pool, page = cache_sb_k.shape[1], cache_sb_k.shape[2]
    tp, ts = bp * lp, bs * ls
    xp = x_prompt.reshape(tp, d)
    xs = x_sample.reshape(ts, d)
    c_sb_k = cache_sb_k.reshape(depth, n_pool, page * ATT_HEADS, ATT_DH)
    c_sb_v = cache_sb_v.reshape(depth, n_pool, page * ATT_HEADS, ATT_DH)
    c_fx_k = cache_fox_k.reshape(depth, n_pool, page * ATT_HEADS, ATT_DH)
    c_fx_v = cache_fox_v.reshape(depth, n_pool, page * ATT_HEADS, ATT_DH)
    c_lf_t = jnp.swapaxes(cache_fox_logf, 2, 3)
    state_t = jnp.swapaxes(state_gla, 3, 4)
    g_fin = g_final.reshape(1, d)

    outs_p, outs_s = [], []
    for l in range(depth):
        w_packed, w_vt = _pack_w_in(w_in[l])
        wlr = jnp.zeros((LANE, GLA_KW), F32).at[TAIL_LR:TAIL_LR + GLA_LOWRANK].set(w_gla_lr[l])
        wlr_hi = wlr.astype(BF16)
        wlr_lo = (wlr - wlr_hi.astype(F32)).astype(BF16)
        blr = b_gla_lr[l].reshape(1, GLA_KW)
        bff = jnp.zeros((1, LANE), F32).at[0, TAIL_FF:TAIL_FF + ATT_HEADS].set(b_fox_f[l])
        g_mix_l = g_mix[l].reshape(1, d)
        g_out = g_gla_out[l].reshape(1, GLA_DV)
        w_g, w_s, w_f, w_o_l = (w.astype(BF16) for w in (w_br_gla[l], w_br_sb[l], w_br_fox[l], w_o[l]))
        ffn_w = (g_ffn[l].reshape(1, d), w_ffn_in[l].astype(BF16), w_conv[l], b_conv[l].reshape(1, d_ff),
                 w_down[l].astype(BF16), g_fin)
        final = l == depth - 1

        pp = _project(xp, g_mix_l, w_packed, w_vt, wlr_hi, wlr_lo, blr, bff)
        logf_p = pp["tail"][:, TAIL_FF:TAIL_FF + ATT_HEADS]
        ckp, cqa = _seq_cumsum(pp["tail"], bp, lp)
        o_sb = _prompt_attention("sb", pp["sqm"], pp["skb"], pp["svt"], bp, lp)
        o_fx = _prompt_attention("fox", pp["fqm"], pp["fkb"], pp["fvt"], bp, lp, cqa, ckp)
        o_gla, st_p = _gla(pp["gq"], pp["gk"], pp["gv"], pp["la"], pp["gg"], g_out, bp, lp)
        xp = _merge(xp, o_gla, o_sb, o_fx, pp["gates"], w_g, w_s, w_f, w_o_l)
        xp, conv_p = _conv_ffn(xp, *ffn_w, final, bp, lp)
        outs_p.append((jnp.swapaxes(st_p, 2, 3), pp["sk"], pp["sv"], pp["fk"], pp["fv"], logf_p, conv_p))

        ps = _project(xs, g_mix_l, w_packed, w_vt, wlr_hi, wlr_lo, blr, bff)
        logf_s = ps["tail"][:, TAIL_FF:TAIL_FF + ATT_HEADS]
        lfn_t = jnp.swapaxes(logf_s.reshape(bs, ls, ATT_HEADS), 1, 2)
        lfn_t = jnp.pad(lfn_t, ((0, 0), (0, 0), (0, LANE - ls)))
        o_sb, o_fx = _decode_attention(l, page_table, ps["sq"], ps["fq"], ps["sk"], ps["sv"], ps["fk"], ps["fv"],
                                       lfn_t, c_sb_k, c_sb_v, c_fx_k, c_fx_v, c_lf_t)
        o_gla, st_s = _gla(ps["gq"], ps["gk"], ps["gv"], ps["la"], ps["gg"], g_out, bs, ls, state_t[l])
        xs = _merge(xs, o_gla, o_sb, o_fx, ps["gates"], w_g, w_s, w_f, w_o_l)
        xs, u_s = _conv_ffn(xs, *ffn_w, final, bs, ls, state_ffn_conv[l])
        conv_s = u_s.reshape(bs, ls, d_ff)[:, ls - (CONV_W - 1):]
        outs_s.append((jnp.swapaxes(st_s, 2, 3), ps["sk"], ps["sv"], ps["fk"], ps["fv"], logf_s, conv_s))

    def stack(outs, i, shape):
        return jnp.stack([o[i].reshape(shape) for o in outs])

    kv_p, kv_s = (bp, lp, ATT_HEADS, ATT_DH), (bs, ls, ATT_HEADS, ATT_DH)
    st_shape_p, st_shape_s = (bp, GLA_HEADS, GLA_DK, GLA_DV), (bs, GLA_HEADS, GLA_DK, GLA_DV)
    return (xp.reshape(bp, lp, d), xs.reshape(bs, ls, d),
            stack(outs_p, 0, st_shape_p), stack(outs_s, 0, st_shape_s),
            stack(outs_p, 1, kv_p), stack(outs_p, 2, kv_p), stack(outs_s, 1, kv_s), stack(outs_s, 2, kv_s),
            stack(outs_p, 3, kv_p), stack(outs_p, 4, kv_p), stack(outs_p, 5, (bp, lp, ATT_HEADS)),
            stack(outs_s, 3, kv_s), stack(outs_s, 4, kv_s), stack(outs_s, 5, (bs, ls, ATT_HEADS)),
            stack(outs_p, 6, (bp, CONV_W - 1, d_ff)), stack(outs_s, 6, (bs, CONV_W - 1, d_ff)))
```

```python
import functools

import numpy as np
import jax
import jax.numpy as jnp
from jax import lax
from jax.experimental import pallas as pl
from jax.experimental.pallas import tpu as pltpu

F32, BF16 = jnp.float32, jnp.bfloat16

GLA_HEADS, GLA_DK, GLA_DV, GLA_LOWRANK, GLA_TAU, GLA_CHUNK = 4, 64, 128, 16, 16.0, 16
ATT_HEADS, ATT_DH = 8, 64
NORM_EPS = 1e-6
CONV_W = 3
GLA_KW = GLA_HEADS * GLA_DK
GLA_VW = GLA_HEADS * GLA_DV
ATT_W = ATT_HEADS * ATT_DH
IN_NAMES = ("gq", "gk", "gv", "glr", "gg", "sq", "sk", "sv", "fq", "fk", "fv", "ff", "gates")

LANE = 128
SUBLANE = 8
VMEM_LIMIT_BYTES = 52 * 1024 * 1024

PACK_ORDER = ("gq", "gk", "gv", "gg", "sq", "sk", "sv", "fq", "fk", "fv", "gates")
TAIL_LR = 0
TAIL_FF = GLA_LOWRANK

PROJ_TM = 256
ATT_T = 256
GLA_TL = 256
MERGE_TM = 512
FFN_TM = 512
FFN_FB = 256
CUMSUM_TB = 512
DEC_NP = 4


def _in_sizes(d_model):
    return dict(gq=GLA_KW, gk=GLA_KW, gv=GLA_VW, glr=GLA_LOWRANK, gg=GLA_VW, sq=ATT_W, sk=ATT_W, sv=ATT_W,
                fq=ATT_W, fk=ATT_W, fv=ATT_W, ff=ATT_HEADS, gates=3 * d_model)


def _pack_layout(d_model):
    sizes = _in_sizes(d_model)
    off, lay = 0, {}
    for n in PACK_ORDER:
        lay[n] = (off, off + sizes[n])
        off += sizes[n]
    lay["tail"] = (off, off + LANE)
    return lay, off + LANE


def _dot(a, b):
    return jnp.dot(a, b, preferred_element_type=F32)


def _dot_nt(a, b):
    return lax.dot_general(a, b, (((1,), (1,)), ((), ())), preferred_element_type=F32)


def _split(a, n):
    parts, r = [], a
    for i in range(n):
        p = r.astype(BF16)
        parts.append(p)
        if i + 1 < n:
            r = r - p.astype(F32)
    return parts


def _split_dot_l(a, b_bf, n):
    out = None
    for p in _split(a, n):
        t = _dot(p, b_bf)
        out = t if out is None else out + t
    return out


def _split_dot_r(a_bf, b, n):
    out = None
    for p in _split(b, n):
        t = _dot(a_bf, p)
        out = t if out is None else out + t
    return out


def _log_sigmoid(x):
    return jnp.minimum(x, 0.0) - jnp.log1p(jnp.exp(-jnp.abs(x)))


def _neg_softplus(x):
    return -(jnp.maximum(x, 0.0) + jnp.log(1.0 + jnp.exp(-jnp.abs(x))))


def _sigmoid(x):
    return 1.0 / (1.0 + jnp.exp(-x))


def _rmsnorm(x, g):
    return x * lax.rsqrt(jnp.mean(x * x, axis=-1, keepdims=True) + NORM_EPS) * g


def _params(*sem):
    return pltpu.CompilerParams(dimension_semantics=sem, vmem_limit_bytes=VMEM_LIMIT_BYTES)


def _resident(shape):
    nd = len(shape)
    return pl.BlockSpec(shape, lambda *_: (0,) * nd, pipeline_mode=pl.Buffered(1))


PROJ_COMMON = ("gq", "gk", "gv", "gg", "gates", "la", "tail")
PROJ_LONG = PROJ_COMMON + ("sq", "fq", "sqm", "fqm", "skb", "fkb", "skt", "svt", "fkt", "fvt", "svtb", "fvtb")
PROJ_SHORT = PROJ_COMMON + ("sq", "sk", "sv", "fq", "fk", "fv")


def _proj_kernel(lay, long_seq, x_ref, g_ref, w_ref, wlr_hi_ref, wlr_lo_ref, blr_ref, bff_ref, *out_refs):
    if long_seq:
        out_refs, stage_ref = out_refs[:-1], out_refs[-1]
    outs = dict(zip(PROJ_LONG if long_seq else PROJ_SHORT, out_refs))
    h = _rmsnorm(x_ref[...], g_ref[...]).astype(BF16)

    def seg(name):
        a, b = lay[name]
        return _dot(h, w_ref[:, a:b])

    scale = ATT_DH ** -0.5
    for n in PACK_ORDER:
        z = seg(n)
        if n in ("gq", "sq", "fq"):
            z = z * scale
        if n in outs:
            outs[n][...] = z
        if not long_seq:
            continue
        if n in ("sq", "fq"):
            low = lax.broadcasted_iota(jnp.int32, (z.shape[0], LANE), 1) < ATT_DH
            for hd in range(ATT_HEADS):
                pair = z[:, (hd // 2) * LANE:(hd // 2 + 1) * LANE]
                own = low if hd % 2 == 0 else jnp.logical_not(low)
                outs[n + "m"][:, hd * LANE:(hd + 1) * LANE] = jnp.where(own, pair, 0.0).astype(BF16)
        if n in ("sk", "fk", "sv", "fv"):
            stage_ref[...] = z
            zt = stage_ref[...].T
            outs[n + "t"][...] = zt
            if n in ("sk", "fk"):
                outs[n + "b"][...] = z.astype(BF16)
            else:
                outs[n + "tb"][...] = zt.astype(BF16)
    tail = seg("tail")
    t_hi, t_lo = _split(tail, 2)
    xlr = _dot(t_hi, wlr_hi_ref[...]) + _dot(t_lo, wlr_hi_ref[...]) + _dot(t_hi, wlr_lo_ref[...])
    outs["la"][...] = _log_sigmoid(xlr + blr_ref[...]) * (1.0 / GLA_TAU)
    outs["tail"][...] = _log_sigmoid(tail + bff_ref[...])


def _project(x, g, w_packed, wlr_hi, wlr_lo, blr, bff, n_seq, seq_len, long_seq):
    t, d = x.shape
    lay, width = _pack_layout(d)
    tm = min(PROJ_TM, t)
    assert t % tm == 0
    names = PROJ_LONG if long_seq else PROJ_SHORT
    widths = {n: lay[n][1] - lay[n][0] for n in PACK_ORDER}
    widths.update(la=GLA_KW, tail=LANE, sqm=ATT_HEADS * LANE, fqm=ATT_HEADS * LANE, skb=ATT_W, fkb=ATT_W)
    dtypes = {n: BF16 for n in ("sqm", "fqm", "skb", "fkb", "svtb", "fvtb")}
    specs, shapes = [], []
    for n in names:
        if n in widths:
            specs.append(pl.BlockSpec((tm, widths[n]), lambda i: (i, 0)))
            shapes.append(jax.ShapeDtypeStruct((t, widths[n]), dtypes.get(n, F32)))
        else:
            assert seq_len % tm == 0
            tiles = seq_len // tm
            specs.append(pl.BlockSpec((None, ATT_W, tm), lambda i: (i // tiles, 0, i % tiles)))
            shapes.append(jax.ShapeDtypeStruct((n_seq, ATT_W, seq_len), dtypes.get(n, F32)))
    outs = pl.pallas_call(
        functools.partial(_proj_kernel, lay, long_seq),
        grid=(t // tm,),
        in_specs=[pl.BlockSpec((tm, d), lambda i: (i, 0)), _resident((1, d)), _resident((d, width)),
                  _resident((LANE, GLA_KW)), _resident((LANE, GLA_KW)), _resident((1, GLA_KW)), _resident((1, LANE))],
        out_specs=specs,
        out_shape=shapes,
        scratch_shapes=[pltpu.VMEM((tm, ATT_W), F32)] if long_seq else [],
        compiler_params=_params("arbitrary"),
        name="project",
    )(x, g, w_packed, wlr_hi, wlr_lo, blr, bff)
    return dict(zip(names, outs))


C_PIECES = 3
C_ONES = C_PIECES * ATT_HEADS


def _cumsum_kernel(x_ref, tri_ref, sk_ref, sq_ref, qconst_ref, ckp_ref, cqa_ref, carry_ref):
    @pl.when(pl.program_id(1) == 0)
    def _():
        carry_ref[...] = jnp.zeros_like(carry_ref)

    c = _split_dot_r(tri_ref[...], x_ref[...], 3) + carry_ref[...]
    carry_ref[...] = c[-1:, :]
    ck, cq = None, None
    for j, p in enumerate(_split(c, C_PIECES)):
        tk, tq = _dot(p, sk_ref[j]), _dot(p, sq_ref[j])
        ck = tk if ck is None else ck + tk
        cq = tq if cq is None else cq + tq
    lane = lax.broadcasted_iota(jnp.int32, ck.shape, 1)
    ones = jnp.logical_and(lane >= C_ONES, lane < C_ONES + C_PIECES)
    ckp_ref[...] = jnp.where(ones, 1.0, -ck).astype(BF16)
    cqa_ref[...] = (cq + qconst_ref[...]).astype(BF16)


def _seq_cumsum(x, n_seq, seq_len):
    tb = min(CUMSUM_TB, seq_len)
    assert seq_len % tb == 0
    nt = seq_len // tb
    tri = jnp.asarray(np.tril(np.ones((tb, tb), np.float32)), BF16)
    sel_k = np.zeros((C_PIECES, LANE, LANE), np.float32)
    sel_q = np.zeros((C_PIECES, LANE, ATT_HEADS * LANE), np.float32)
    qconst = np.zeros((1, ATT_HEADS * LANE), np.float32)
    for h in range(ATT_HEADS):
        for j in range(C_PIECES):
            sel_k[j, TAIL_FF + h, C_PIECES * h + j] = 1.0
            sel_q[j, TAIL_FF + h, h * LANE + C_ONES + j] = 1.0
            qconst[0, h * LANE + C_PIECES * h + j] = 1.0
    row = lambda w: pl.BlockSpec((tb, w), lambda b, i: (b * nt + i, 0))
    return pl.pallas_call(
        _cumsum_kernel,
        grid=(n_seq, nt),
        in_specs=[row(LANE), _resident((tb, tb)), _resident(sel_k.shape), _resident(sel_q.shape),
                  _resident(qconst.shape)],
        out_specs=[row(LANE), row(ATT_HEADS * LANE)],
        out_shape=[jax.ShapeDtypeStruct((x.shape[0], LANE), BF16),
                   jax.ShapeDtypeStruct((x.shape[0], ATT_HEADS * LANE), BF16)],
        scratch_shapes=[pltpu.VMEM((1, LANE), F32)],
        compiler_params=_params("arbitrary", "arbitrary"),
        name="logf_cumsum",
    )(x, tri, jnp.asarray(sel_k, BF16), jnp.asarray(sel_q, BF16), jnp.asarray(qconst))


def _tri_tables(nq):
    qi = [q for q in range(nq) for _ in range(q + 1)]
    kj = [k for q in range(nq) for k in range(q, -1, -1)]
    return jnp.asarray(qi, jnp.int32), jnp.asarray(kj, jnp.int32)


def _sb_prompt_kernel(qi_ref, kj_ref, qm_ref, k_ref, vt_ref, u_ref, o_ref, acc_ref, carry_ref):
    s = pl.program_id(1)
    qi, kj = qi_ref[s], kj_ref[s]
    tk, tq = k_ref.shape[0], qm_ref.shape[0]

    @pl.when(kj == qi)
    def _():
        acc_ref[...] = jnp.zeros_like(acc_ref)
        carry_ref[...] = jnp.zeros_like(carry_ref)

    def step(masked):
        if masked:
            keep = lax.broadcasted_iota(jnp.int32, (tk, tq), 0) < lax.broadcasted_iota(jnp.int32, (tk, tq), 1)
        carry = carry_ref[...]
        heads = range(ATT_HEADS)
        zts = [_dot_nt(k_ref[:, (h // 2) * LANE:(h // 2 + 1) * LANE], qm_ref[:, h * LANE:(h + 1) * LANE])
               for h in heads]
        lks = [_neg_softplus(zt) for zt in zts]
        if masked:
            lks = [jnp.where(keep, lk, 0.0) for lk in lks]
        sfxs = [_split_dot_r(u_ref[...], lk, 2) for lk in lks]
        ws = [jnp.exp(zts[h] + sfxs[h] + carry[h:h + 1, :]) for h in heads]
        if masked:
            ws = [jnp.where(keep, w, 0.0) for w in ws]
        for h in heads:
            rows = slice(h * ATT_DH, (h + 1) * ATT_DH)
            acc_ref[rows, :] += _dot(vt_ref[rows, :], ws[h].astype(BF16))
        carry_ref[...] = carry + jnp.concatenate([sfx[0:1, :] for sfx in sfxs], axis=0)

    pl.when(kj == qi)(functools.partial(step, True))
    pl.when(kj != qi)(functools.partial(step, False))

    @pl.when(kj == 0)
    def _():
        o_ref[...] = acc_ref[...].T.astype(o_ref.dtype)


def _fox_prompt_kernel(qi_ref, kj_ref, qm_ref, cqa_ref, k_ref, ckp_ref, vt_ref, o_ref, acc_ref, m_ref, l_ref):
    s = pl.program_id(1)
    qi, kj = qi_ref[s], kj_ref[s]
    tk, tq = k_ref.shape[0], qm_ref.shape[0]

    @pl.when(kj == qi)
    def _():
        acc_ref[...] = jnp.zeros_like(acc_ref)
        l_ref[...] = jnp.zeros_like(l_ref)
        m_ref[...] = jnp.full_like(m_ref, -jnp.inf)

    def step(masked):
        if masked:
            keep = lax.broadcasted_iota(jnp.int32, (tk, tq), 0) <= lax.broadcasted_iota(jnp.int32, (tk, tq), 1)
        m_prev, l_prev = m_ref[...], l_ref[...]
        heads = range(ATT_HEADS)
        zts = []
        for h in heads:
            pair = slice((h // 2) * LANE, (h // 2 + 1) * LANE)
            blk = slice(h * LANE, (h + 1) * LANE)
            ka = jnp.concatenate([k_ref[:, pair], ckp_ref[...]], axis=1)
            qa = jnp.concatenate([qm_ref[:, blk], cqa_ref[:, blk]], axis=1)
            zts.append(_dot_nt(ka, qa))
        if masked:
            zts = [jnp.where(keep, zt, -jnp.inf) for zt in zts]
        m_new = jnp.maximum(m_prev, jnp.concatenate([jnp.max(zt, axis=0, keepdims=True) for zt in zts], axis=0))
        alpha = jnp.exp(m_prev - m_new)
        ps = [jnp.exp(zts[h] - m_new[h:h + 1, :]) for h in heads]
        l_ref[...] = alpha * l_prev + jnp.concatenate([jnp.sum(p, axis=0, keepdims=True) for p in ps], axis=0)
        m_ref[...] = m_new
        for h in heads:
            rows = slice(h * ATT_DH, (h + 1) * ATT_DH)
            acc_ref[rows, :] = alpha[h:h + 1, :] * acc_ref[rows, :] + _dot(vt_ref[rows, :], ps[h].astype(BF16))

    pl.when(kj == qi)(functools.partial(step, True))
    pl.when(kj != qi)(functools.partial(step, False))

    @pl.when(kj == 0)
    def _():
        l_all = l_ref[...]
        inv = jnp.concatenate([jnp.broadcast_to(1.0 / l_all[h:h + 1, :], (ATT_DH, tq)) for h in range(ATT_HEADS)],
                              axis=0)
        o_ref[...] = (acc_ref[...] * inv).T.astype(o_ref.dtype)


def _prompt_attention(kind, qm, k, vt, n_seq, seq_len, cqa=None, ckp=None):
    t = min(ATT_T, seq_len)
    assert seq_len % t == 0
    nq = seq_len // t
    qi_tab, kj_tab = _tri_tables(nq)
    n_steps = int(qi_tab.shape[0])
    qrow = lambda w: pl.BlockSpec((t, w), lambda b, s, qi, kj: (b * nq + qi[s], 0))
    krow = lambda w: pl.BlockSpec((t, w), lambda b, s, qi, kj: (b * nq + kj[s], 0))
    vt_spec = pl.BlockSpec((None, ATT_W, t), lambda b, s, qi, kj: (b, 0, kj[s]))
    stat = pltpu.VMEM((ATT_HEADS, t), F32)
    acc = pltpu.VMEM((ATT_W, t), F32)
    if kind == "sb":
        u = jnp.asarray(np.triu(np.ones((t, t), np.float32)), BF16)
        body, args = _sb_prompt_kernel, (qm, k, vt, u)
        in_specs = [qrow(ATT_HEADS * LANE), krow(ATT_W), vt_spec, pl.BlockSpec((t, t), lambda b, s, qi, kj: (0, 0))]
        scratch = [acc, stat]
    else:
        body, args = _fox_prompt_kernel, (qm, cqa, k, ckp, vt)
        in_specs = [qrow(ATT_HEADS * LANE), qrow(ATT_HEADS * LANE), krow(ATT_W), krow(LANE), vt_spec]
        scratch = [acc, stat, stat]
    return pl.pallas_call(
        body,
        grid_spec=pltpu.PrefetchScalarGridSpec(
            num_scalar_prefetch=2, grid=(n_seq, n_steps),
            in_specs=in_specs, out_specs=qrow(ATT_W), scratch_shapes=scratch),
        out_shape=jax.ShapeDtypeStruct((n_seq * seq_len, ATT_W), BF16),
        compiler_params=_params("arbitrary", "arbitrary"),
        name=kind + "_prompt_attention",
    )(qi_tab, kj_tab, *args)


def _rep8(x):
    return jnp.concatenate([jnp.broadcast_to(x[h:h + 1, :], (SUBLANE, x.shape[1])) for h in range(ATT_HEADS)],
                           axis=0)


def _fold_heads(a):
    hm = (lax.broadcasted_iota(jnp.int32, a.shape, 1) // ATT_DH) == (lax.broadcasted_iota(jnp.int32, a.shape, 0) // SUBLANE)
    am = jnp.where(hm, a, 0.0)
    out = am[0:SUBLANE]
    for h in range(1, ATT_HEADS):
        out = out + am[h * SUBLANE:(h + 1) * SUBLANE]
    return out


def _page_cols(page_refs):
    return jnp.concatenate([r[...] for r in page_refs], axis=1).astype(BF16)


def _decode_kernel(n_p, n_groups, pt_ref, qs_ref, qf_ref, ksn_ref, vsn_ref, kfn_ref, vfn_ref, lfn_ref, *rest):
    ks, vs, kf, vf, lf = (rest[i * n_p:(i + 1) * n_p] for i in range(5))
    usi_ref, use_ref, upi_ref, osb_ref, ofx_ref = rest[5 * n_p:5 * n_p + 5]
    (qbs_ref, qbf_ref, accs_ref, accf_ref, cs_ref, m_ref, l_ref, cd_ref, cq_ref,
     pks_ref, pvs_ref, pkf_ref, pvf_ref) = rest[5 * n_p + 5:]
    del pt_ref
    b, g = pl.program_id(0), pl.program_id(1)
    nq = qs_ref.shape[0]
    rows = ATT_HEADS * nq
    page = LANE
    row = lax.broadcasted_iota(jnp.int32, (rows, page), 0)
    lane = lax.broadcasted_iota(jnp.int32, (rows, page), 1)
    t_of_row = row % nq

    def attend(kt_s, vt_s, kt_f, vt_f, bias_f, mask_s, mask_f):
        z = _dot(qbs_ref[...], kt_s)
        zf = _dot(qbf_ref[...], kt_f) + bias_f
        lk = _neg_softplus(z)
        if mask_s is not None:
            lk = jnp.where(mask_s, lk, 0.0)
        usi = usi_ref[...] if mask_s is None else usi_ref[0:page, 0:page]
        sfx = _split_dot_l(lk, usi, 2)
        c = cs_ref[...]
        w = jnp.exp(z + sfx + c)
        if mask_s is not None:
            w = jnp.where(mask_s, w, 0.0)
        accs_ref[...] += _dot_nt(w.astype(BF16), vt_s)
        cs_ref[...] = c + jnp.sum(lk, axis=1, keepdims=True)

        if mask_f is not None:
            zf = jnp.where(mask_f, zf, -jnp.inf)
        m_prev = m_ref[...]
        m_new = jnp.maximum(m_prev, jnp.max(zf, axis=1, keepdims=True))
        alpha = jnp.exp(m_prev - m_new)
        p = jnp.exp(zf - m_new)
        l_ref[...] = alpha * l_ref[...] + jnp.sum(p, axis=1, keepdims=True)
        accf_ref[...] = alpha * accf_ref[...] + _dot_nt(p.astype(BF16), vt_f)
        m_ref[...] = m_new

    @pl.when(jnp.logical_and(b == 0, g == 0))
    def _():
        for r in (pks_ref, pvs_ref, pkf_ref, pvf_ref):
            r[...] = jnp.zeros_like(r)

    @pl.when(g == 0)
    def _():
        shape = (rows, ATT_W)
        hm = (lax.broadcasted_iota(jnp.int32, shape, 1) // ATT_DH) == (lax.broadcasted_iota(jnp.int32, shape, 0) // nq)
        for q_ref, qb_ref in ((qs_ref, qbs_ref), (qf_ref, qbf_ref)):
            qrep = jnp.concatenate([q_ref[...]] * ATT_HEADS, axis=0)
            qb_ref[...] = jnp.where(hm, qrep, 0.0).astype(BF16)
        accs_ref[...] = jnp.zeros_like(accs_ref)
        accf_ref[...] = jnp.zeros_like(accf_ref)
        cs_ref[...] = jnp.zeros_like(cs_ref)
        l_ref[...] = jnp.zeros_like(l_ref)
        m_ref[...] = jnp.full_like(m_ref, -jnp.inf)
        cd_ref[...] = jnp.zeros_like(cd_ref)
        for src, dst in ((ksn_ref, pks_ref), (vsn_ref, pvs_ref), (kfn_ref, pkf_ref), (vfn_ref, pvf_ref)):
            dst[0:nq, :] = src[...]
        c_new = _rep8(_split_dot_l(lfn_ref[...], upi_ref[...], 3))
        cq = jnp.sum(jnp.where(lane == t_of_row, c_new, 0.0), axis=1, keepdims=True)
        cq_ref[...] = cq
        attend(pks_ref[...].T.astype(BF16), pvs_ref[...].T.astype(BF16), pkf_ref[...].T.astype(BF16),
               pvf_ref[...].T.astype(BF16), cq - c_new, lane < t_of_row, lane <= t_of_row)

    lfp = jnp.concatenate([r[...] for r in lf], axis=1)
    d = _split_dot_l(lfp, use_ref[...], 3) + cd_ref[...]
    cd_ref[...] += jnp.sum(lfp, axis=1, keepdims=True)
    attend(_page_cols(ks), _page_cols(vs), _page_cols(kf), _page_cols(vf), _rep8(d) + cq_ref[...], None, None)

    @pl.when(g == n_groups - 1)
    def _():
        osb_ref[...] = _fold_heads(accs_ref[...])
        ofx_ref[...] = _fold_heads(accf_ref[...] / l_ref[...])


def _decode_attention(layer, page_table, qs, qf, ksn, vsn, kfn, vfn, lfn_t, c_sb_k, c_sb_v, c_fx_k, c_fx_v,
                      c_lf_t):
    n_seq, n_pages = page_table.shape
    nq = qs.shape[0] // n_seq
    assert nq == SUBLANE and c_sb_k.shape[2:] == (ATT_W, LANE)
    n_p = min(DEC_NP, n_pages)
    assert n_pages % n_p == 0
    n_groups = n_pages // n_p
    rows = ATT_HEADS * nq
    keys = n_p * LANE
    ones = np.ones((keys, keys), np.float32)
    usi = jnp.asarray(np.tril(ones), BF16)
    use = jnp.asarray(np.tril(ones, -1), BF16)
    upi = jnp.asarray(np.triu(ones[:LANE, :LANE]), BF16)
    new = pl.BlockSpec((nq, ATT_W), lambda b, g, pt: (b, 0))

    def page_spec(shape, i):
        def imap(b, g, pt):
            return (layer, pt[b * n_pages + n_pages - (g + 1) * n_p + i], 0, 0)
        return pl.BlockSpec((None, None) + shape, imap)

    kv_specs = [page_spec((ATT_W, LANE), i) for i in range(n_p)]
    lf_specs = [page_spec((ATT_HEADS, LANE), i) for i in range(n_p)]
    const = lambda n: pl.BlockSpec((n, n), lambda b, g, pt: (0, 0))
    col = pltpu.VMEM((rows, 1), F32)
    pad = pltpu.VMEM((LANE, ATT_W), F32)
    return pl.pallas_call(
        functools.partial(_decode_kernel, n_p, n_groups),
        grid_spec=pltpu.PrefetchScalarGridSpec(
            num_scalar_prefetch=1, grid=(n_seq, n_groups),
            in_specs=[new] * 6 + [pl.BlockSpec((None, ATT_HEADS, LANE), lambda b, g, pt: (b, 0, 0))]
            + kv_specs * 4 + lf_specs + [const(keys), const(keys), const(LANE)],
            out_specs=[new, new],
            scratch_shapes=[pltpu.VMEM((rows, ATT_W), BF16), pltpu.VMEM((rows, ATT_W), BF16),
                            pltpu.VMEM((rows, ATT_W), F32), pltpu.VMEM((rows, ATT_W), F32),
                            col, col, col, pltpu.VMEM((ATT_HEADS, 1), F32), col, pad, pad, pad, pad]),
        out_shape=[jax.ShapeDtypeStruct(qs.shape, F32)] * 2,
        compiler_params=_params("arbitrary", "arbitrary"),
        name="decode_attention",
    )(page_table.reshape(-1), qs, qf, ksn, vsn, kfn, vfn, lfn_t,
      *([c_sb_k] * n_p), *([c_sb_v] * n_p), *([c_fx_k] * n_p), *([c_fx_v] * n_p), *([c_lf_t] * n_p),
      usi, use, upi)


def _gla_kernel(chunk, per_chunk_state, n_tiles, q_ref, k_ref, v_ref, la_ref, gg_ref, *rest):
    if per_chunk_state:
        s0_ref, rest = rest[0], rest[1:]
    (bdi_ref, bdf_ref, mexp_ref, bmask_ref, gout_ref, o_ref, st_ref,
     kh_ref, bh_ref, vh_ref, qt_ref, kt_ref, eb_ref, vt_ref, oi_ref, ox_ref, sbd_ref) = rest
    tl = q_ref.shape[0]
    n_chunks = tl // chunk
    tile = pl.program_id(1)

    @pl.when(jnp.logical_and(pl.program_id(0) == 0, tile == 0))
    def _():
        for r in (kh_ref, bh_ref, vh_ref):
            r[0:chunk, :] = jnp.zeros((chunk, r.shape[1]), F32)

    @pl.when(tile == 0)
    def _():
        sbd_ref[...] = jnp.zeros_like(sbd_ref)

    q, k, v, la = q_ref[...], k_ref[...], v_ref[...], la_ref[...]
    b = _split_dot_r(bdi_ref[...], la, 3)
    bend = _split_dot_r(bdf_ref[...], la, 3)
    qt_ref[...] = q * jnp.exp(b)
    kt_ref[...] = k * jnp.exp(bend - b)
    eb_ref[...] = jnp.exp(bend)
    vt_ref[...] = v.T.astype(BF16)
    kh_ref[chunk:, :] = k
    bh_ref[chunk:, :] = b
    vh_ref[chunk:, :] = v

    pos = lax.broadcasted_iota(jnp.int32, (tl, GLA_KW), 0) % chunk
    oi_ref[...] = jnp.zeros_like(oi_ref)
    for delta in range(chunk):
        ksh = kh_ref[chunk - delta:chunk - delta + tl, :]
        bsh = bh_ref[chunk - delta:chunk - delta + tl, :]
        vsh = vh_ref[chunk - delta:chunk - delta + tl, :]
        p = jnp.where(pos >= delta, q * ksh * jnp.exp(b - bsh), 0.0)
        a = _split_dot_l(p, mexp_ref[...], 2)
        oi_ref[...] += a * vsh

    rowid = lax.broadcasted_iota(jnp.int32, (tl, GLA_KW), 0)

    def chunk_step(i, carry):
        r = pl.multiple_of(i * chunk, chunk)
        if per_chunk_state:
            for h in range(GLA_HEADS):
                sbd_ref[h * GLA_DV:(h + 1) * GLA_DV, h * GLA_DK:(h + 1) * GLA_DK] = s0_ref[i, h]
        s = sbd_ref[...]
        ox_ref[pl.ds(r, chunk), :] = _dot_nt(qt_ref[pl.ds(r, chunk), :].astype(BF16), s.astype(BF16))
        in_chunk = jnp.logical_and(rowid >= r, rowid < r + chunk)
        km = jnp.where(in_chunk, kt_ref[...], 0.0).astype(BF16)
        ut = _dot(vt_ref[...], km)
        s_new = eb_ref[pl.ds(r, 1), :] * s + ut * bmask_ref[...]
        sbd_ref[...] = s_new
        if per_chunk_state:
            for h in range(GLA_HEADS):
                st_ref[i, h] = s_new[h * GLA_DV:(h + 1) * GLA_DV, h * GLA_DK:(h + 1) * GLA_DK]
        return carry

    lax.fori_loop(0, n_chunks, chunk_step, 0)

    if not per_chunk_state:
        @pl.when(tile == n_tiles - 1)
        def _():
            s = sbd_ref[...]
            for h in range(GLA_HEADS):
                st_ref[0, h] = s[h * GLA_DV:(h + 1) * GLA_DV, h * GLA_DK:(h + 1) * GLA_DK]

    o = oi_ref[...] + ox_ref[...]
    gate = gg_ref[...]
    gate = gate * _sigmoid(gate)
    for h in range(GLA_HEADS):
        sl = slice(h * GLA_DV, (h + 1) * GLA_DV)
        o_ref[:, sl] = (_rmsnorm(o[:, sl], gout_ref[...]) * gate[:, sl]).astype(o_ref.dtype)


def _gla(q, k, v, la, gg, g_out, n_seq, seq_len, s0_t=None):
    t_total = q.shape[0]
    per_chunk_state = s0_t is not None
    chunk = GLA_CHUNK if seq_len % GLA_CHUNK == 0 else seq_len
    tl = min(GLA_TL, t_total)
    assert tl % chunk == 0 and t_total % tl == 0
    if per_chunk_state:
        assert chunk == seq_len
        n_groups, n_tiles, n_st = t_total // tl, 1, tl // chunk
    else:
        assert seq_len % tl == 0
        n_groups, n_tiles, n_st = n_seq, seq_len // tl, 1
    idx = np.arange(tl)
    same = (idx[:, None] // chunk) == (idx[None, :] // chunk)
    bdi = jnp.asarray(same & (idx[None, :] <= idx[:, None]), BF16)
    bdf = jnp.asarray(same, BF16)
    mexp = jnp.asarray((np.arange(GLA_KW)[:, None] // GLA_DK) == (np.arange(GLA_VW)[None, :] // GLA_DV), BF16)
    bmask = jnp.asarray((np.arange(GLA_VW)[:, None] // GLA_DV) == (np.arange(GLA_KW)[None, :] // GLA_DK), F32)
    row = lambda w: pl.BlockSpec((tl, w), lambda s, i: (s * n_tiles + i, 0))
    st_spec = pl.BlockSpec((n_st, GLA_HEADS, GLA_DV, GLA_DK), lambda s, i: (s, 0, 0, 0))
    in_specs = [row(GLA_KW), row(GLA_KW), row(GLA_VW), row(GLA_KW), row(GLA_VW)]
    args = [q, k, v, la, gg]
    if per_chunk_state:
        in_specs.append(st_spec)
        args.append(s0_t)
    in_specs += [_resident((tl, tl)), _resident((tl, tl)), _resident((GLA_KW, GLA_VW)),
                 _resident((GLA_VW, GLA_KW)), _resident((1, GLA_DV))]
    args += [bdi, bdf, mexp, bmask, g_out]
    vm = lambda r, c, dt=F32: pltpu.VMEM((r, c), dt)
    scratch = [vm(tl + chunk, GLA_KW), vm(tl + chunk, GLA_KW), vm(tl + chunk, GLA_VW),
               vm(tl, GLA_KW), vm(tl, GLA_KW), vm(tl, GLA_KW), vm(GLA_VW, tl, BF16),
               vm(tl, GLA_VW), vm(tl, GLA_VW), vm(GLA_VW, GLA_KW)]
    return pl.pallas_call(
        functools.partial(_gla_kernel, chunk, per_chunk_state, n_tiles),
        grid=(n_groups, n_tiles),
        in_specs=in_specs,
        out_specs=[row(GLA_VW), st_spec],
        out_shape=[jax.ShapeDtypeStruct((t_total, GLA_VW), BF16),
                   jax.ShapeDtypeStruct((n_groups * n_st, GLA_HEADS, GLA_DV, GLA_DK), F32)],
        scratch_shapes=scratch,
        compiler_params=_params("arbitrary", "arbitrary"),
        name="gla",
    )(*args)


def _merge_kernel(x_ref, og_ref, os_ref, of_ref, gates_ref, wg_ref, ws_ref, wf_ref, wo_ref, o_ref):
    d = x_ref.shape[1]
    m = None
    for i, (b_ref, w_ref) in enumerate(((og_ref, wg_ref), (os_ref, ws_ref), (of_ref, wf_ref))):
        t = _sigmoid(gates_ref[:, i * d:(i + 1) * d]) * _dot(b_ref[...].astype(BF16), w_ref[...])
        m = t if m is None else m + t
    o_ref[...] = x_ref[...] + _dot(m.astype(BF16), wo_ref[...])


def _merge(x, o_gla, o_sb, o_fx, gates, w_g, w_s, w_f, w_o):
    t, d = x.shape
    tm = min(MERGE_TM, t)
    assert t % tm == 0
    row = lambda w: pl.BlockSpec((tm, w), lambda i: (i, 0))
    return pl.pallas_call(
        _merge_kernel,
        grid=(t // tm,),
        in_specs=[row(d), row(GLA_VW), row(ATT_W), row(ATT_W), row(3 * d), _resident(w_g.shape),
                  _resident(w_s.shape), _resident(w_f.shape), _resident(w_o.shape)],
        out_specs=row(d),
        out_shape=jax.ShapeDtypeStruct(x.shape, F32),
        compiler_params=_params("arbitrary"),
        name="merge",
    )(x, o_gla, o_sb, o_fx, gates, w_g, w_s, w_f, w_o)


def _ffn_kernel(carry_state, final_norm, tiles_per_seq, seq_rows, x_ref, g_ref, win_ref, wc_ref, bc_ref, wd_ref,
                gfin_ref, *rest):
    if carry_state:
        o_ref, conv_ref, act_ref, carry_ref = rest
    else:
        p1_ref, p2_ref, o_ref, u_ref, act_ref = rest
    tm = x_ref.shape[0]
    d_ff = wd_ref.shape[0]
    x = x_ref[...]
    h = _rmsnorm(x, g_ref[...]).astype(BF16)
    if carry_state:
        @pl.when(pl.program_id(0) % tiles_per_seq == 0)
        def _():
            carry_ref[...] = jnp.zeros_like(carry_ref)

    for j in range(d_ff // FFN_FB):
        sl = slice(j * FFN_FB, (j + 1) * FFN_FB)
        u = _dot(h, win_ref[:, sl])
        gt = _dot(h, win_ref[:, d_ff + j * FFN_FB:d_ff + (j + 1) * FFN_FB])
        pos = lax.broadcasted_iota(jnp.int32, u.shape, 0)
        r1 = pltpu.roll(u, 1, 0)
        r2 = pltpu.roll(u, 2, 0)
        if carry_state:
            c0 = carry_ref[0:1, sl]
            c1 = carry_ref[1:2, sl]
            u1 = jnp.where(pos == 0, c1, r1)
            u2 = jnp.where(pos == 0, c0, jnp.where(pos == 1, c1, r2))
            carry_ref[:, sl] = u[tm - (CONV_W - 1):, :]
            conv_ref[:, sl] = u[tm - (CONV_W - 1):, :]
        else:
            pos = pos % seq_rows
            u1 = jnp.where(pos == 0, p1_ref[:, sl], r1)
            u2 = jnp.where(pos < 2, p2_ref[:, sl], r2)
            u_ref[:, sl] = u
        uc = bc_ref[:, sl] + wc_ref[0:1, sl] * u2 + wc_ref[1:2, sl] * u1 + wc_ref[2:3, sl] * u
        act_ref[:, sl] = (uc * _sigmoid(uc) * gt).astype(BF16)
    y = x + _dot(act_ref[...], wd_ref[...])
    if final_norm:
        y = _rmsnorm(y, gfin_ref[...])
    o_ref[...] = y


def _conv_ffn(x, g, w_in, w_conv, b_conv, w_down, g_final, final_norm, n_seq, seq_len, past=None):
    t, d = x.shape
    d_ff = w_down.shape[0]
    assert d_ff % FFN_FB == 0 and CONV_W == 3
    carry_state = past is None
    tm = min(FFN_TM if carry_state else FFN_TM // 2, t)
    assert t % tm == 0
    row = lambda w: pl.BlockSpec((tm, w), lambda i: (i, 0))
    in_specs = [row(d), _resident((1, d)), _resident(w_in.shape), _resident(w_conv.shape), _resident((1, d_ff)),
                _resident(w_down.shape), _resident((1, d))]
    args = [x, g, w_in, w_conv, b_conv, w_down, g_final]
    scratch = [pltpu.VMEM((tm, d_ff), BF16)]
    if carry_state:
        assert seq_len % tm == 0
        tiles_per_seq = seq_len // tm
        out_specs = [row(d), pl.BlockSpec((None, CONV_W - 1, d_ff), lambda i: (i // tiles_per_seq, 0, 0))]
        out_shape = [jax.ShapeDtypeStruct(x.shape, F32), jax.ShapeDtypeStruct((n_seq, CONV_W - 1, d_ff), F32)]
        scratch.append(pltpu.VMEM((CONV_W - 1, d_ff), F32))
    else:
        assert seq_len == SUBLANE
        tiles_per_seq = 0
        pad = jnp.zeros((n_seq, seq_len, d_ff), F32)
        p1 = pad.at[:, 0].set(past[:, 1]).reshape(t, d_ff)
        p2 = pad.at[:, 0:2].set(past).reshape(t, d_ff)
        in_specs += [row(d_ff), row(d_ff)]
        args += [p1, p2]
        out_specs = [row(d), row(d_ff)]
        out_shape = [jax.ShapeDtypeStruct(x.shape, F32), jax.ShapeDtypeStruct((t, d_ff), F32)]
    return pl.pallas_call(
        functools.partial(_ffn_kernel, carry_state, final_norm, tiles_per_seq, seq_len),
        grid=(t // tm,),
        in_specs=in_specs,
        out_specs=out_specs,
        out_shape=out_shape,
        scratch_shapes=scratch,
        compiler_params=_params("arbitrary"),
        name="conv_ffn",
    )(*args)


def _pack_w_in(w):
    d = w.shape[0]
    sizes = _in_sizes(d)
    offs, o = {}, 0
    for n in IN_NAMES:
        offs[n] = (o, o + sizes[n])
        o += sizes[n]
    seg = lambda n: w[:, offs[n][0]:offs[n][1]]
    tail = jnp.zeros((d, LANE), w.dtype)
    tail = tail.at[:, TAIL_LR:TAIL_LR + GLA_LOWRANK].set(seg("glr"))
    tail = tail.at[:, TAIL_FF:TAIL_FF + ATT_HEADS].set(seg("ff"))
    return jnp.concatenate([seg(n) for n in PACK_ORDER] + [tail], axis=1).astype(BF16)


def kernel(x_prompt, x_sample, state_gla, cache_sb_k, cache_sb_v, cache_fox_k, cache_fox_v, cache_fox_logf,
           state_ffn_conv, page_table, g_mix, w_in, w_gla_lr, b_gla_lr, g_gla_out, b_fox_f, w_br_gla, w_br_sb,
           w_br_fox, w_o, g_ffn, w_ffn_in, w_conv, b_conv, w_down, g_final):
    bp, lp, d = x_prompt.shape
    bs, ls, _ = x_sample.shape
    depth = w_in.shape[0]
    d_ff = w_down.shape[1]
    n_pool, page = cache_sb_k.shape[1], cache_sb_k.shape[2]
    tp, ts = bp * lp, bs * ls
    xp = x_prompt.reshape(tp, d)
    xs = x_sample.reshape(ts, d)
    page_t = lambda c: jnp.transpose(c, (0, 1, 3, 4, 2)).reshape(depth, n_pool, ATT_W, page)
    c_sb_k, c_sb_v, c_fx_k, c_fx_v = (page_t(c) for c in (cache_sb_k, cache_sb_v, cache_fox_k, cache_fox_v))
    c_lf_t = jnp.swapaxes(cache_fox_logf, 2, 3)
    state_t = jnp.swapaxes(state_gla, 3, 4)
    g_fin = g_final.reshape(1, d)

    outs_p, outs_s = [], []
    for l in range(depth):
        w_packed = _pack_w_in(w_in[l])
        wlr = jnp.zeros((LANE, GLA_KW), F32).at[TAIL_LR:TAIL_LR + GLA_LOWRANK].set(w_gla_lr[l])
        wlr_hi = wlr.astype(BF16)
        wlr_lo = (wlr - wlr_hi.astype(F32)).astype(BF16)
        blr = b_gla_lr[l].reshape(1, GLA_KW)
        bff = jnp.zeros((1, LANE), F32).at[0, TAIL_FF:TAIL_FF + ATT_HEADS].set(b_fox_f[l])
        g_mix_l = g_mix[l].reshape(1, d)
        g_out = g_gla_out[l].reshape(1, GLA_DV)
        w_g, w_s, w_f, w_o_l = (w.astype(BF16) for w in (w_br_gla[l], w_br_sb[l], w_br_fox[l], w_o[l]))
        ffn_w = (g_ffn[l].reshape(1, d), w_ffn_in[l].astype(BF16), w_conv[l], b_conv[l].reshape(1, d_ff),
                 w_down[l].astype(BF16), g_fin)
        final = l == depth - 1

        pp = _project(xp, g_mix_l, w_packed, wlr_hi, wlr_lo, blr, bff, bp, lp, True)
        logf_p = pp["tail"][:, TAIL_FF:TAIL_FF + ATT_HEADS]
        ckp, cqa = _seq_cumsum(pp["tail"], bp, lp)
        o_sb = _prompt_attention("sb", pp["sqm"], pp["skb"], pp["svtb"], bp, lp)
        o_fx = _prompt_attention("fox", pp["fqm"], pp["fkb"], pp["fvtb"], bp, lp, cqa, ckp)
        o_gla, st_p = _gla(pp["gq"], pp["gk"], pp["gv"], pp["la"], pp["gg"], g_out, bp, lp)
        xp = _merge(xp, o_gla, o_sb, o_fx, pp["gates"], w_g, w_s, w_f, w_o_l)
        xp, conv_p = _conv_ffn(xp, *ffn_w, final, bp, lp)
        rows_p = [jnp.transpose(pp[n].reshape(bp, ATT_HEADS, ATT_DH, lp), (0, 3, 1, 2))
                  for n in ("skt", "svt", "fkt", "fvt")]
        outs_p.append((jnp.swapaxes(st_p, 2, 3), *rows_p, logf_p, conv_p))

        ps = _project(xs, g_mix_l, w_packed, wlr_hi, wlr_lo, blr, bff, bs, ls, False)
        logf_s = ps["tail"][:, TAIL_FF:TAIL_FF + ATT_HEADS]
        lfn_t = jnp.swapaxes(logf_s.reshape(bs, ls, ATT_HEADS), 1, 2)
        lfn_t = jnp.pad(lfn_t, ((0, 0), (0, 0), (0, LANE - ls)))
        o_sb, o_fx = _decode_attention(l, page_table, ps["sq"], ps["fq"], ps["sk"], ps["sv"], ps["fk"], ps["fv"],
                                       lfn_t, c_sb_k, c_sb_v, c_fx_k, c_fx_v, c_lf_t)
        o_gla, st_s = _gla(ps["gq"], ps["gk"], ps["gv"], ps["la"], ps["gg"], g_out, bs, ls, state_t[l])
        xs = _merge(xs, o_gla, o_sb, o_fx, ps["gates"], w_g, w_s, w_f, w_o_l)
        xs, u_s = _conv_ffn(xs, *ffn_w, final, bs, ls, state_ffn_conv[l])
        conv_s = u_s.reshape(bs, ls, d_ff)[:, ls - (CONV_W - 1):]
        outs_s.append((jnp.swapaxes(st_s, 2, 3), ps["sk"], ps["sv"], ps["fk"], ps["fv"], logf_s, conv_s))

    def stack(outs, i, shape):
        return jnp.stack([o[i].reshape(shape) for o in outs])

    kv_p, kv_s = (bp, lp, ATT_HEADS, ATT_DH), (bs, ls, ATT_HEADS, ATT_DH)
    st_shape_p, st_shape_s = (bp, GLA_HEADS, GLA_DK, GLA_DV), (bs, GLA_HEADS, GLA_DK, GLA_DV)
    return (xp.reshape(bp, lp, d), xs.reshape(bs, ls, d),
            stack(outs_p, 0, st_shape_p), stack(outs_s, 0, st_shape_s),
            stack(outs_p, 1, kv_p), stack(outs_p, 2, kv_p), stack(outs_s, 1, kv_s), stack(outs_s, 2, kv_s),
            stack(outs_p, 3, kv_p), stack(outs_p, 4, kv_p), stack(outs_p, 5, (bp, lp, ATT_HEADS)),
            stack(outs_s, 3, kv_s), stack(outs_s, 4, kv_s), stack(outs_s, 5, (bs, ls, ATT_HEADS)),
            stack(outs_p, 6, (bp, CONV_W - 1, d_ff)), stack(outs_s, 6, (bs, CONV_W - 1, d_ff)))
```

```python
import functools

import numpy as np
import jax
import jax.numpy as jnp
from jax import lax
from jax.experimental import pallas as pl
from jax.experimental.pallas import tpu as pltpu

F32, BF16 = jnp.float32, jnp.bfloat16

GLA_HEADS, GLA_DK, GLA_DV, GLA_LOWRANK, GLA_TAU, GLA_CHUNK = 4, 64, 128, 16, 16.0, 16
ATT_HEADS, ATT_DH = 8, 64
NORM_EPS = 1e-6
CONV_W = 3
GLA_KW = GLA_HEADS * GLA_DK
GLA_VW = GLA_HEADS * GLA_DV
ATT_W = ATT_HEADS * ATT_DH
IN_NAMES = ("gq", "gk", "gv", "glr", "gg", "sq", "sk", "sv", "fq", "fk", "fv", "ff", "gates")

LANE = 128
SUBLANE = 8
VMEM_LIMIT_BYTES = 52 * 1024 * 1024

PACK_ORDER = ("gq", "gk", "gv", "gg", "sq", "sk", "sv", "fq", "fk", "fv", "gates")
TAIL_LR = 0
TAIL_FF = GLA_LOWRANK

PROJ_TM = 256
ATT_T = 256
GLA_TL = 256
MERGE_TM = 512
FFN_TM = 512
FFN_FB = 256
CUMSUM_TB = 512
DEC_NP = 4
SB_SUFFIX_PIECES = 1


def _in_sizes(d_model):
    return dict(gq=GLA_KW, gk=GLA_KW, gv=GLA_VW, glr=GLA_LOWRANK, gg=GLA_VW, sq=ATT_W, sk=ATT_W, sv=ATT_W,
                fq=ATT_W, fk=ATT_W, fv=ATT_W, ff=ATT_HEADS, gates=3 * d_model)


def _pack_layout(d_model):
    sizes = _in_sizes(d_model)
    off, lay = 0, {}
    for n in PACK_ORDER:
        lay[n] = (off, off + sizes[n])
        off += sizes[n]
    lay["tail"] = (off, off + LANE)
    return lay, off + LANE


def _dot(a, b):
    return jnp.dot(a, b, preferred_element_type=F32)


def _dot_nt(a, b):
    return lax.dot_general(a, b, (((1,), (1,)), ((), ())), preferred_element_type=F32)


def _split(a, n):
    parts, r = [], a
    for i in range(n):
        p = r.astype(BF16)
        parts.append(p)
        if i + 1 < n:
            r = r - p.astype(F32)
    return parts


def _split_dot_l(a, b_bf, n):
    out = None
    for p in _split(a, n):
        t = _dot(p, b_bf)
        out = t if out is None else out + t
    return out


def _split_dot_r(a_bf, b, n):
    out = None
    for p in _split(b, n):
        t = _dot(a_bf, p)
        out = t if out is None else out + t
    return out


def _log_sigmoid(x):
    return jnp.minimum(x, 0.0) - jnp.log1p(jnp.exp(-jnp.abs(x)))


def _neg_softplus(x):
    return -(jnp.maximum(x, 0.0) + jnp.log(1.0 + jnp.exp(-jnp.abs(x))))


def _sigmoid(x):
    return 1.0 / (1.0 + jnp.exp(-x))


def _rmsnorm(x, g):
    return x * lax.rsqrt(jnp.mean(x * x, axis=-1, keepdims=True) + NORM_EPS) * g


def _params(*sem):
    return pltpu.CompilerParams(dimension_semantics=sem, vmem_limit_bytes=VMEM_LIMIT_BYTES)


def _resident(shape):
    nd = len(shape)
    return pl.BlockSpec(shape, lambda *_: (0,) * nd, pipeline_mode=pl.Buffered(1))


PROJ_COMMON = ("gq", "gk", "gv", "gg", "gates", "la", "tail")
PROJ_LONG = PROJ_COMMON + ("sq", "fq", "sqm", "fqm", "skb", "fkb", "skt", "svt", "fkt", "fvt", "svtb", "fvtb")
PROJ_SHORT = PROJ_COMMON + ("sq", "sk", "sv", "fq", "fk", "fv")


def _proj_kernel(lay, long_seq, x_ref, g_ref, w_ref, wlr_hi_ref, wlr_lo_ref, blr_ref, bff_ref, *out_refs):
    if long_seq:
        out_refs, stage_ref = out_refs[:-1], out_refs[-1]
    outs = dict(zip(PROJ_LONG if long_seq else PROJ_SHORT, out_refs))
    h = _rmsnorm(x_ref[...], g_ref[...]).astype(BF16)

    def seg(name):
        a, b = lay[name]
        return _dot(h, w_ref[:, a:b])

    scale = ATT_DH ** -0.5
    for n in PACK_ORDER:
        z = seg(n)
        if n in ("gq", "sq", "fq"):
            z = z * scale
        if n in outs:
            outs[n][...] = z
        if not long_seq:
            continue
        if n in ("sq", "fq"):
            low = lax.broadcasted_iota(jnp.int32, (z.shape[0], LANE), 1) < ATT_DH
            for hd in range(ATT_HEADS):
                pair = z[:, (hd // 2) * LANE:(hd // 2 + 1) * LANE]
                own = low if hd % 2 == 0 else jnp.logical_not(low)
                outs[n + "m"][:, hd * LANE:(hd + 1) * LANE] = jnp.where(own, pair, 0.0).astype(BF16)
        if n in ("sk", "fk", "sv", "fv"):
            stage_ref[...] = z
            zt = stage_ref[...].T
            outs[n + "t"][...] = zt
            if n in ("sk", "fk"):
                outs[n + "b"][...] = z.astype(BF16)
            else:
                outs[n + "tb"][...] = zt.astype(BF16)
    tail = seg("tail")
    t_hi, t_lo = _split(tail, 2)
    xlr = _dot(t_hi, wlr_hi_ref[...]) + _dot(t_lo, wlr_hi_ref[...]) + _dot(t_hi, wlr_lo_ref[...])
    outs["la"][...] = _log_sigmoid(xlr + blr_ref[...]) * (1.0 / GLA_TAU)
    outs["tail"][...] = _log_sigmoid(tail + bff_ref[...])


def _project(x, g, w_packed, wlr_hi, wlr_lo, blr, bff, n_seq, seq_len, long_seq):
    t, d = x.shape
    lay, width = _pack_layout(d)
    tm = min(PROJ_TM, t)
    assert t % tm == 0
    names = PROJ_LONG if long_seq else PROJ_SHORT
    widths = {n: lay[n][1] - lay[n][0] for n in PACK_ORDER}
    widths.update(la=GLA_KW, tail=LANE, sqm=ATT_HEADS * LANE, fqm=ATT_HEADS * LANE, skb=ATT_W, fkb=ATT_W)
    dtypes = {n: BF16 for n in ("sqm", "fqm", "skb", "fkb", "svtb", "fvtb")}
    specs, shapes = [], []
    for n in names:
        if n in widths:
            specs.append(pl.BlockSpec((tm, widths[n]), lambda i: (i, 0)))
            shapes.append(jax.ShapeDtypeStruct((t, widths[n]), dtypes.get(n, F32)))
        else:
            assert seq_len % tm == 0
            tiles = seq_len // tm
            specs.append(pl.BlockSpec((None, ATT_W, tm), lambda i: (i // tiles, 0, i % tiles)))
            shapes.append(jax.ShapeDtypeStruct((n_seq, ATT_W, seq_len), dtypes.get(n, F32)))
    outs = pl.pallas_call(
        functools.partial(_proj_kernel, lay, long_seq),
        grid=(t // tm,),
        in_specs=[pl.BlockSpec((tm, d), lambda i: (i, 0)), _resident((1, d)), _resident((d, width)),
                  _resident((LANE, GLA_KW)), _resident((LANE, GLA_KW)), _resident((1, GLA_KW)), _resident((1, LANE))],
        out_specs=specs,
        out_shape=shapes,
        scratch_shapes=[pltpu.VMEM((tm, ATT_W), F32)] if long_seq else [],
        compiler_params=_params("arbitrary"),
        name="project",
    )(x, g, w_packed, wlr_hi, wlr_lo, blr, bff)
    return dict(zip(names, outs))


C_PIECES = 3
C_ONES = C_PIECES * ATT_HEADS


def _cumsum_kernel(x_ref, tri_ref, sk_ref, sq_ref, qconst_ref, ckp_ref, cqa_ref, carry_ref):
    @pl.when(pl.program_id(1) == 0)
    def _():
        carry_ref[...] = jnp.zeros_like(carry_ref)

    c = _split_dot_r(tri_ref[...], x_ref[...], 3) + carry_ref[...]
    carry_ref[...] = c[-1:, :]
    ck, cq = None, None
    for j, p in enumerate(_split(c, C_PIECES)):
        tk, tq = _dot(p, sk_ref[j]), _dot(p, sq_ref[j])
        ck = tk if ck is None else ck + tk
        cq = tq if cq is None else cq + tq
    lane = lax.broadcasted_iota(jnp.int32, ck.shape, 1)
    ones = jnp.logical_and(lane >= C_ONES, lane < C_ONES + C_PIECES)
    ckp_ref[...] = jnp.where(ones, 1.0, -ck).astype(BF16)
    cqa_ref[...] = (cq + qconst_ref[...]).astype(BF16)


def _seq_cumsum(x, n_seq, seq_len):
    tb = min(CUMSUM_TB, seq_len)
    assert seq_len % tb == 0
    nt = seq_len // tb
    tri = jnp.asarray(np.tril(np.ones((tb, tb), np.float32)), BF16)
    sel_k = np.zeros((C_PIECES, LANE, LANE), np.float32)
    sel_q = np.zeros((C_PIECES, LANE, ATT_HEADS * LANE), np.float32)
    qconst = np.zeros((1, ATT_HEADS * LANE), np.float32)
    for h in range(ATT_HEADS):
        for j in range(C_PIECES):
            sel_k[j, TAIL_FF + h, C_PIECES * h + j] = 1.0
            sel_q[j, TAIL_FF + h, h * LANE + C_ONES + j] = 1.0
            qconst[0, h * LANE + C_PIECES * h + j] = 1.0
    row = lambda w: pl.BlockSpec((tb, w), lambda b, i: (b * nt + i, 0))
    return pl.pallas_call(
        _cumsum_kernel,
        grid=(n_seq, nt),
        in_specs=[row(LANE), _resident((tb, tb)), _resident(sel_k.shape), _resident(sel_q.shape),
                  _resident(qconst.shape)],
        out_specs=[row(LANE), row(ATT_HEADS * LANE)],
        out_shape=[jax.ShapeDtypeStruct((x.shape[0], LANE), BF16),
                   jax.ShapeDtypeStruct((x.shape[0], ATT_HEADS * LANE), BF16)],
        scratch_shapes=[pltpu.VMEM((1, LANE), F32)],
        compiler_params=_params("arbitrary", "arbitrary"),
        name="logf_cumsum",
    )(x, tri, jnp.asarray(sel_k, BF16), jnp.asarray(sel_q, BF16), jnp.asarray(qconst))


def _tri_tables(nq):
    qi = [q for q in range(nq) for _ in range(q + 1)]
    kj = [k for q in range(nq) for k in range(q, -1, -1)]
    return jnp.asarray(qi, jnp.int32), jnp.asarray(kj, jnp.int32)


def _sb_prompt_kernel(qi_ref, kj_ref, qm_ref, k_ref, vt_ref, u_ref, o_ref, acc_ref, carry_ref):
    s = pl.program_id(1)
    qi, kj = qi_ref[s], kj_ref[s]
    tk, tq = k_ref.shape[0], qm_ref.shape[0]

    @pl.when(kj == qi)
    def _():
        acc_ref[...] = jnp.zeros_like(acc_ref)
        carry_ref[...] = jnp.zeros_like(carry_ref)

    def step(masked):
        if masked:
            keep = lax.broadcasted_iota(jnp.int32, (tk, tq), 0) < lax.broadcasted_iota(jnp.int32, (tk, tq), 1)
        carry = carry_ref[...]
        heads = range(ATT_HEADS)
        zts = [_dot_nt(k_ref[:, (h // 2) * LANE:(h // 2 + 1) * LANE], qm_ref[:, h * LANE:(h + 1) * LANE])
               for h in heads]
        lks = [_neg_softplus(zt) for zt in zts]
        if masked:
            lks = [jnp.where(keep, lk, 0.0) for lk in lks]
        sfxs = [_split_dot_r(u_ref[...], lk, SB_SUFFIX_PIECES) for lk in lks]
        ws = [jnp.exp(zts[h] + sfxs[h] + carry[h:h + 1, :]) for h in heads]
        if masked:
            ws = [jnp.where(keep, w, 0.0) for w in ws]
        for h in heads:
            rows = slice(h * ATT_DH, (h + 1) * ATT_DH)
            acc_ref[rows, :] += _dot(vt_ref[rows, :], ws[h].astype(BF16))
        carry_ref[...] = carry + jnp.concatenate([sfx[0:1, :] for sfx in sfxs], axis=0)

    pl.when(kj == qi)(functools.partial(step, True))
    pl.when(kj != qi)(functools.partial(step, False))

    @pl.when(kj == 0)
    def _():
        o_ref[...] = acc_ref[...].T.astype(o_ref.dtype)


def _fox_prompt_kernel(qi_ref, kj_ref, qm_ref, cqa_ref, k_ref, ckp_ref, vt_ref, o_ref, acc_ref, m_ref, l_ref):
    s = pl.program_id(1)
    qi, kj = qi_ref[s], kj_ref[s]
    tk, tq = k_ref.shape[0], qm_ref.shape[0]

    @pl.when(kj == qi)
    def _():
        acc_ref[...] = jnp.zeros_like(acc_ref)
        l_ref[...] = jnp.zeros_like(l_ref)
        m_ref[...] = jnp.full_like(m_ref, -jnp.inf)

    def step(masked):
        if masked:
            keep = lax.broadcasted_iota(jnp.int32, (tk, tq), 0) <= lax.broadcasted_iota(jnp.int32, (tk, tq), 1)
        m_prev, l_prev = m_ref[...], l_ref[...]
        heads = range(ATT_HEADS)
        zts = []
        for h in heads:
            pair = slice((h // 2) * LANE, (h // 2 + 1) * LANE)
            blk = slice(h * LANE, (h + 1) * LANE)
            ka = jnp.concatenate([k_ref[:, pair], ckp_ref[...]], axis=1)
            qa = jnp.concatenate([qm_ref[:, blk], cqa_ref[:, blk]], axis=1)
            zts.append(_dot_nt(ka, qa))
        if masked:
            zts = [jnp.where(keep, zt, -jnp.inf) for zt in zts]
        m_new = jnp.maximum(m_prev, jnp.concatenate([jnp.max(zt, axis=0, keepdims=True) for zt in zts], axis=0))
        alpha = jnp.exp(m_prev - m_new)
        ps = [jnp.exp(zts[h] - m_new[h:h + 1, :]) for h in heads]
        l_ref[...] = alpha * l_prev + jnp.concatenate([jnp.sum(p, axis=0, keepdims=True) for p in ps], axis=0)
        m_ref[...] = m_new
        for h in heads:
            rows = slice(h * ATT_DH, (h + 1) * ATT_DH)
            acc_ref[rows, :] = alpha[h:h + 1, :] * acc_ref[rows, :] + _dot(vt_ref[rows, :], ps[h].astype(BF16))

    pl.when(kj == qi)(functools.partial(step, True))
    pl.when(kj != qi)(functools.partial(step, False))

    @pl.when(kj == 0)
    def _():
        l_all = l_ref[...]
        inv = jnp.concatenate([jnp.broadcast_to(1.0 / l_all[h:h + 1, :], (ATT_DH, tq)) for h in range(ATT_HEADS)],
                              axis=0)
        o_ref[...] = (acc_ref[...] * inv).T.astype(o_ref.dtype)


def _prompt_attention_kernel(qi_ref, kj_ref, sqm_ref, sk_ref, svt_ref, u_ref, fqm_ref, cqa_ref, fk_ref, ckp_ref,
                             fvt_ref, osb_ref, ofx_ref, sacc_ref, carry_ref, facc_ref, m_ref, l_ref):
    _sb_prompt_kernel(qi_ref, kj_ref, sqm_ref, sk_ref, svt_ref, u_ref, osb_ref, sacc_ref, carry_ref)
    _fox_prompt_kernel(qi_ref, kj_ref, fqm_ref, cqa_ref, fk_ref, ckp_ref, fvt_ref, ofx_ref, facc_ref, m_ref, l_ref)


def _prompt_attention(sqm, sk, svt, fqm, cqa, fk, ckp, fvt, n_seq, seq_len):
    t = min(ATT_T, seq_len)
    assert seq_len % t == 0
    nq = seq_len // t
    qi_tab, kj_tab = _tri_tables(nq)
    n_steps = int(qi_tab.shape[0])
    qrow = lambda w: pl.BlockSpec((t, w), lambda b, s, qi, kj: (b * nq + qi[s], 0))
    krow = lambda w: pl.BlockSpec((t, w), lambda b, s, qi, kj: (b * nq + kj[s], 0))
    vt_spec = pl.BlockSpec((None, ATT_W, t), lambda b, s, qi, kj: (b, 0, kj[s]))
    stat = pltpu.VMEM((ATT_HEADS, t), F32)
    acc = pltpu.VMEM((ATT_W, t), F32)
    u = jnp.asarray(np.triu(np.ones((t, t), np.float32)), BF16)
    in_specs = [qrow(ATT_HEADS * LANE), krow(ATT_W), vt_spec, pl.BlockSpec((t, t), lambda b, s, qi, kj: (0, 0)),
                qrow(ATT_HEADS * LANE), qrow(ATT_HEADS * LANE), krow(ATT_W), krow(LANE), vt_spec]
    out = jax.ShapeDtypeStruct((n_seq * seq_len, ATT_W), BF16)
    return pl.pallas_call(
        _prompt_attention_kernel,
        grid_spec=pltpu.PrefetchScalarGridSpec(
            num_scalar_prefetch=2, grid=(n_seq, n_steps),
            in_specs=in_specs, out_specs=[qrow(ATT_W), qrow(ATT_W)],
            scratch_shapes=[acc, stat, acc, stat, stat]),
        out_shape=[out, out],
        compiler_params=_params("arbitrary", "arbitrary"),
        name="prompt_attention",
    )(qi_tab, kj_tab, sqm, sk, svt, u, fqm, cqa, fk, ckp, fvt)


def _rep8(x):
    return jnp.concatenate([jnp.broadcast_to(x[h:h + 1, :], (SUBLANE, x.shape[1])) for h in range(ATT_HEADS)],
                           axis=0)


def _fold_heads(a):
    hm = (lax.broadcasted_iota(jnp.int32, a.shape, 1) // ATT_DH) == (lax.broadcasted_iota(jnp.int32, a.shape, 0) // SUBLANE)
    am = jnp.where(hm, a, 0.0)
    out = am[0:SUBLANE]
    for h in range(1, ATT_HEADS):
        out = out + am[h * SUBLANE:(h + 1) * SUBLANE]
    return out


def _page_cols(page_refs):
    return jnp.concatenate([r[...] for r in page_refs], axis=1).astype(BF16)


def _decode_kernel(n_p, n_groups, pt_ref, qs_ref, qf_ref, ksn_ref, vsn_ref, kfn_ref, vfn_ref, lfn_ref, *rest):
    ks, vs, kf, vf, lf = (rest[i * n_p:(i + 1) * n_p] for i in range(5))
    usi_ref, use_ref, upi_ref, osb_ref, ofx_ref = rest[5 * n_p:5 * n_p + 5]
    (qbs_ref, qbf_ref, accs_ref, accf_ref, cs_ref, m_ref, l_ref, cd_ref, cq_ref,
     pks_ref, pvs_ref, pkf_ref, pvf_ref) = rest[5 * n_p + 5:]
    del pt_ref
    b, g = pl.program_id(0), pl.program_id(1)
    nq = qs_ref.shape[0]
    rows = ATT_HEADS * nq
    page = LANE
    row = lax.broadcasted_iota(jnp.int32, (rows, page), 0)
    lane = lax.broadcasted_iota(jnp.int32, (rows, page), 1)
    t_of_row = row % nq

    def attend(kt_s, vt_s, kt_f, vt_f, bias_f, mask_s, mask_f):
        z = _dot(qbs_ref[...], kt_s)
        zf = _dot(qbf_ref[...], kt_f) + bias_f
        lk = _neg_softplus(z)
        if mask_s is not None:
            lk = jnp.where(mask_s, lk, 0.0)
        usi = usi_ref[...] if mask_s is None else usi_ref[0:page, 0:page]
        sfx = _split_dot_l(lk, usi, 2)
        c = cs_ref[...]
        w = jnp.exp(z + sfx + c)
        if mask_s is not None:
            w = jnp.where(mask_s, w, 0.0)
        accs_ref[...] += _dot_nt(w.astype(BF16), vt_s)
        cs_ref[...] = c + jnp.sum(lk, axis=1, keepdims=True)

        if mask_f is not None:
            zf = jnp.where(mask_f, zf, -jnp.inf)
        m_prev = m_ref[...]
        m_new = jnp.maximum(m_prev, jnp.max(zf, axis=1, keepdims=True))
        alpha = jnp.exp(m_prev - m_new)
        p = jnp.exp(zf - m_new)
        l_ref[...] = alpha * l_ref[...] + jnp.sum(p, axis=1, keepdims=True)
        accf_ref[...] = alpha * accf_ref[...] + _dot_nt(p.astype(BF16), vt_f)
        m_ref[...] = m_new

    @pl.when(jnp.logical_and(b == 0, g == 0))
    def _():
        for r in (pks_ref, pvs_ref, pkf_ref, pvf_ref):
            r[...] = jnp.zeros_like(r)

    @pl.when(g == 0)
    def _():
        shape = (rows, ATT_W)
        hm = (lax.broadcasted_iota(jnp.int32, shape, 1) // ATT_DH) == (lax.broadcasted_iota(jnp.int32, shape, 0) // nq)
        for q_ref, qb_ref in ((qs_ref, qbs_ref), (qf_ref, qbf_ref)):
            qrep = jnp.concatenate([q_ref[...]] * ATT_HEADS, axis=0)
            qb_ref[...] = jnp.where(hm, qrep, 0.0).astype(BF16)
        accs_ref[...] = jnp.zeros_like(accs_ref)
        accf_ref[...] = jnp.zeros_like(accf_ref)
        cs_ref[...] = jnp.zeros_like(cs_ref)
        l_ref[...] = jnp.zeros_like(l_ref)
        m_ref[...] = jnp.full_like(m_ref, -jnp.inf)
        cd_ref[...] = jnp.zeros_like(cd_ref)
        for src, dst in ((ksn_ref, pks_ref), (vsn_ref, pvs_ref), (kfn_ref, pkf_ref), (vfn_ref, pvf_ref)):
            dst[0:nq, :] = src[...]
        c_new = _rep8(_split_dot_l(lfn_ref[...], upi_ref[...], 3))
        cq = jnp.sum(jnp.where(lane == t_of_row, c_new, 0.0), axis=1, keepdims=True)
        cq_ref[...] = cq
        attend(pks_ref[...].T.astype(BF16), pvs_ref[...].T.astype(BF16), pkf_ref[...].T.astype(BF16),
               pvf_ref[...].T.astype(BF16), cq - c_new, lane < t_of_row, lane <= t_of_row)

    lfp = jnp.concatenate([r[...] for r in lf], axis=1)
    d = _split_dot_l(lfp, use_ref[...], 3) + cd_ref[...]
    cd_ref[...] += jnp.sum(lfp, axis=1, keepdims=True)
    attend(_page_cols(ks), _page_cols(vs), _page_cols(kf), _page_cols(vf), _rep8(d) + cq_ref[...], None, None)

    @pl.when(g == n_groups - 1)
    def _():
        osb_ref[...] = _fold_heads(accs_ref[...])
        ofx_ref[...] = _fold_heads(accf_ref[...] / l_ref[...])


def _decode_attention(layer, page_table, qs, qf, ksn, vsn, kfn, vfn, lfn_t, c_sb_k, c_sb_v, c_fx_k, c_fx_v,
                      c_lf_t):
    n_seq, n_pages = page_table.shape
    nq = qs.shape[0] // n_seq
    assert nq == SUBLANE and c_sb_k.shape[2:] == (ATT_W, LANE)
    n_p = min(DEC_NP, n_pages)
    assert n_pages % n_p == 0
    n_groups = n_pages // n_p
    rows = ATT_HEADS * nq
    keys = n_p * LANE
    ones = np.ones((keys, keys), np.float32)
    usi = jnp.asarray(np.tril(ones), BF16)
    use = jnp.asarray(np.tril(ones, -1), BF16)
    upi = jnp.asarray(np.triu(ones[:LANE, :LANE]), BF16)
    new = pl.BlockSpec((nq, ATT_W), lambda b, g, pt: (b, 0))

    def page_spec(shape, i):
        def imap(b, g, pt):
            return (layer, pt[b * n_pages + n_pages - (g + 1) * n_p + i], 0, 0)
        return pl.BlockSpec((None, None) + shape, imap)

    kv_specs = [page_spec((ATT_W, LANE), i) for i in range(n_p)]
    lf_specs = [page_spec((ATT_HEADS, LANE), i) for i in range(n_p)]
    const = lambda n: pl.BlockSpec((n, n), lambda b, g, pt: (0, 0))
    col = pltpu.VMEM((rows, 1), F32)
    pad = pltpu.VMEM((LANE, ATT_W), F32)
    return pl.pallas_call(
        functools.partial(_decode_kernel, n_p, n_groups),
        grid_spec=pltpu.PrefetchScalarGridSpec(
            num_scalar_prefetch=1, grid=(n_seq, n_groups),
            in_specs=[new] * 6 + [pl.BlockSpec((None, ATT_HEADS, LANE), lambda b, g, pt: (b, 0, 0))]
            + kv_specs * 4 + lf_specs + [const(keys), const(keys), const(LANE)],
            out_specs=[new, new],
            scratch_shapes=[pltpu.VMEM((rows, ATT_W), BF16), pltpu.VMEM((rows, ATT_W), BF16),
                            pltpu.VMEM((rows, ATT_W), F32), pltpu.VMEM((rows, ATT_W), F32),
                            col, col, col, pltpu.VMEM((ATT_HEADS, 1), F32), col, pad, pad, pad, pad]),
        out_shape=[jax.ShapeDtypeStruct(qs.shape, F32)] * 2,
        compiler_params=_params("arbitrary", "arbitrary"),
        name="decode_attention",
    )(page_table.reshape(-1), qs, qf, ksn, vsn, kfn, vfn, lfn_t,
      *([c_sb_k] * n_p), *([c_sb_v] * n_p), *([c_fx_k] * n_p), *([c_fx_v] * n_p), *([c_lf_t] * n_p),
      usi, use, upi)


def _gla_kernel(chunk, per_chunk_state, n_tiles, q_ref, k_ref, v_ref, la_ref, gg_ref, *rest):
    if per_chunk_state:
        s0_ref, rest = rest[0], rest[1:]
    (bdi_ref, bdf_ref, mexp_ref, bmask_ref, gout_ref, o_ref, st_ref,
     kh_ref, bh_ref, vh_ref, qt_ref, kt_ref, eb_ref, vt_ref, oi_ref, ox_ref, sbd_ref) = rest
    tl = q_ref.shape[0]
    n_chunks = tl // chunk
    tile = pl.program_id(1)

    @pl.when(jnp.logical_and(pl.program_id(0) == 0, tile == 0))
    def _():
        for r in (kh_ref, bh_ref, vh_ref):
            r[0:chunk, :] = jnp.zeros((chunk, r.shape[1]), F32)

    @pl.when(tile == 0)
    def _():
        sbd_ref[...] = jnp.zeros_like(sbd_ref)

    q, k, v, la = q_ref[...], k_ref[...], v_ref[...], la_ref[...]
    b = _split_dot_r(bdi_ref[...], la, 3)
    bend = _split_dot_r(bdf_ref[...], la, 3)
    qt_ref[...] = q * jnp.exp(b)
    kt_ref[...] = k * jnp.exp(bend - b)
    eb_ref[...] = jnp.exp(bend)
    vt_ref[...] = v.T.astype(BF16)
    kh_ref[chunk:, :] = k
    bh_ref[chunk:, :] = b
    vh_ref[chunk:, :] = v

    pos = lax.broadcasted_iota(jnp.int32, (tl, GLA_KW), 0) % chunk
    oi_ref[...] = jnp.zeros_like(oi_ref)
    for delta in range(chunk):
        ksh = kh_ref[chunk - delta:chunk - delta + tl, :]
        bsh = bh_ref[chunk - delta:chunk - delta + tl, :]
        vsh = vh_ref[chunk - delta:chunk - delta + tl, :]
        p = jnp.where(pos >= delta, q * ksh * jnp.exp(b - bsh), 0.0)
        a = _split_dot_l(p, mexp_ref[...], 2)
        oi_ref[...] += a * vsh

    rowid = lax.broadcasted_iota(jnp.int32, (tl, GLA_KW), 0)

    def chunk_step(i, carry):
        r = pl.multiple_of(i * chunk, chunk)
        if per_chunk_state:
            for h in range(GLA_HEADS):
                sbd_ref[h * GLA_DV:(h + 1) * GLA_DV, h * GLA_DK:(h + 1) * GLA_DK] = s0_ref[i, h]
        s = sbd_ref[...]
        ox_ref[pl.ds(r, chunk), :] = _dot_nt(qt_ref[pl.ds(r, chunk), :].astype(BF16), s.astype(BF16))
        in_chunk = jnp.logical_and(rowid >= r, rowid < r + chunk)
        km = jnp.where(in_chunk, kt_ref[...], 0.0).astype(BF16)
        ut = _dot(vt_ref[...], km)
        s_new = eb_ref[pl.ds(r, 1), :] * s + ut * bmask_ref[...]
        sbd_ref[...] = s_new
        if per_chunk_state:
            for h in range(GLA_HEADS):
                st_ref[i, h] = s_new[h * GLA_DV:(h + 1) * GLA_DV, h * GLA_DK:(h + 1) * GLA_DK]
        return carry

    lax.fori_loop(0, n_chunks, chunk_step, 0, unroll=True)

    if not per_chunk_state:
        @pl.when(tile == n_tiles - 1)
        def _():
            s = sbd_ref[...]
            for h in range(GLA_HEADS):
                st_ref[0, h] = s[h * GLA_DV:(h + 1) * GLA_DV, h * GLA_DK:(h + 1) * GLA_DK]

    o = oi_ref[...] + ox_ref[...]
    gate = gg_ref[...]
    gate = gate * _sigmoid(gate)
    for h in range(GLA_HEADS):
        sl = slice(h * GLA_DV, (h + 1) * GLA_DV)
        o_ref[:, sl] = (_rmsnorm(o[:, sl], gout_ref[...]) * gate[:, sl]).astype(o_ref.dtype)


def _gla(q, k, v, la, gg, g_out, n_seq, seq_len, s0_t=None):
    t_total = q.shape[0]
    per_chunk_state = s0_t is not None
    chunk = GLA_CHUNK if seq_len % GLA_CHUNK == 0 else seq_len
    tl = min(GLA_TL, t_total)
    assert tl % chunk == 0 and t_total % tl == 0
    if per_chunk_state:
        assert chunk == seq_len
        n_groups, n_tiles, n_st = t_total // tl, 1, tl // chunk
    else:
        assert seq_len % tl == 0
        n_groups, n_tiles, n_st = n_seq, seq_len // tl, 1
    idx = np.arange(tl)
    same = (idx[:, None] // chunk) == (idx[None, :] // chunk)
    bdi = jnp.asarray(same & (idx[None, :] <= idx[:, None]), BF16)
    bdf = jnp.asarray(same, BF16)
    mexp = jnp.asarray((np.arange(GLA_KW)[:, None] // GLA_DK) == (np.arange(GLA_VW)[None, :] // GLA_DV), BF16)
    bmask = jnp.asarray((np.arange(GLA_VW)[:, None] // GLA_DV) == (np.arange(GLA_KW)[None, :] // GLA_DK), F32)
    row = lambda w: pl.BlockSpec((tl, w), lambda s, i: (s * n_tiles + i, 0))
    st_spec = pl.BlockSpec((n_st, GLA_HEADS, GLA_DV, GLA_DK), lambda s, i: (s, 0, 0, 0))
    in_specs = [row(GLA_KW), row(GLA_KW), row(GLA_VW), row(GLA_KW), row(GLA_VW)]
    args = [q, k, v, la, gg]
    if per_chunk_state:
        in_specs.append(st_spec)
        args.append(s0_t)
    in_specs += [_resident((tl, tl)), _resident((tl, tl)), _resident((GLA_KW, GLA_VW)),
                 _resident((GLA_VW, GLA_KW)), _resident((1, GLA_DV))]
    args += [bdi, bdf, mexp, bmask, g_out]
    vm = lambda r, c, dt=F32: pltpu.VMEM((r, c), dt)
    scratch = [vm(tl + chunk, GLA_KW), vm(tl + chunk, GLA_KW), vm(tl + chunk, GLA_VW),
               vm(tl, GLA_KW), vm(tl, GLA_KW), vm(tl, GLA_KW), vm(GLA_VW, tl, BF16),
               vm(tl, GLA_VW), vm(tl, GLA_VW), vm(GLA_VW, GLA_KW)]
    return pl.pallas_call(
        functools.partial(_gla_kernel, chunk, per_chunk_state, n_tiles),
        grid=(n_groups, n_tiles),
        in_specs=in_specs,
        out_specs=[row(GLA_VW), st_spec],
        out_shape=[jax.ShapeDtypeStruct((t_total, GLA_VW), BF16),
                   jax.ShapeDtypeStruct((n_groups * n_st, GLA_HEADS, GLA_DV, GLA_DK), F32)],
        scratch_shapes=scratch,
        compiler_params=_params("arbitrary", "arbitrary"),
        name="gla",
    )(*args)


def _merge_kernel(x_ref, og_ref, os_ref, of_ref, gates_ref, wg_ref, ws_ref, wf_ref, wo_ref, o_ref):
    d = x_ref.shape[1]
    m = None
    for i, (b_ref, w_ref) in enumerate(((og_ref, wg_ref), (os_ref, ws_ref), (of_ref, wf_ref))):
        t = _sigmoid(gates_ref[:, i * d:(i + 1) * d]) * _dot(b_ref[...].astype(BF16), w_ref[...])
        m = t if m is None else m + t
    o_ref[...] = x_ref[...] + _dot(m.astype(BF16), wo_ref[...])


def _merge(x, o_gla, o_sb, o_fx, gates, w_g, w_s, w_f, w_o):
    t, d = x.shape
    tm = min(MERGE_TM, t)
    assert t % tm == 0
    row = lambda w: pl.BlockSpec((tm, w), lambda i: (i, 0))
    return pl.pallas_call(
        _merge_kernel,
        grid=(t // tm,),
        in_specs=[row(d), row(GLA_VW), row(ATT_W), row(ATT_W), row(3 * d), _resident(w_g.shape),
                  _resident(w_s.shape), _resident(w_f.shape), _resident(w_o.shape)],
        out_specs=row(d),
        out_shape=jax.ShapeDtypeStruct(x.shape, F32),
        compiler_params=_params("arbitrary"),
        name="merge",
    )(x, o_gla, o_sb, o_fx, gates, w_g, w_s, w_f, w_o)


def _ffn_kernel(carry_state, final_norm, tiles_per_seq, seq_rows, x_ref, g_ref, win_ref, wc_ref, bc_ref, wd_ref,
                gfin_ref, *rest):
    if carry_state:
        o_ref, conv_ref, act_ref, carry_ref = rest
    else:
        p1_ref, p2_ref, o_ref, u_ref, act_ref = rest
    tm = x_ref.shape[0]
    d_ff = wd_ref.shape[0]
    x = x_ref[...]
    h = _rmsnorm(x, g_ref[...]).astype(BF16)
    if carry_state:
        @pl.when(pl.program_id(0) % tiles_per_seq == 0)
        def _():
            carry_ref[...] = jnp.zeros_like(carry_ref)

    for j in range(d_ff // FFN_FB):
        sl = slice(j * FFN_FB, (j + 1) * FFN_FB)
        u = _dot(h, win_ref[:, sl])
        gt = _dot(h, win_ref[:, d_ff + j * FFN_FB:d_ff + (j + 1) * FFN_FB])
        pos = lax.broadcasted_iota(jnp.int32, u.shape, 0)
        r1 = pltpu.roll(u, 1, 0)
        r2 = pltpu.roll(u, 2, 0)
        if carry_state:
            c0 = carry_ref[0:1, sl]
            c1 = carry_ref[1:2, sl]
            u1 = jnp.where(pos == 0, c1, r1)
            u2 = jnp.where(pos == 0, c0, jnp.where(pos == 1, c1, r2))
            carry_ref[:, sl] = u[tm - (CONV_W - 1):, :]
            conv_ref[:, sl] = u[tm - (CONV_W - 1):, :]
        else:
            pos = pos % seq_rows
            u1 = jnp.where(pos == 0, p1_ref[:, sl], r1)
            u2 = jnp.where(pos < 2, p2_ref[:, sl], r2)
            u_ref[:, sl] = u
        uc = bc_ref[:, sl] + wc_ref[0:1, sl] * u2 + wc_ref[1:2, sl] * u1 + wc_ref[2:3, sl] * u
        act_ref[:, sl] = (uc * _sigmoid(uc) * gt).astype(BF16)
    y = x + _dot(act_ref[...], wd_ref[...])
    if final_norm:
        y = _rmsnorm(y, gfin_ref[...])
    o_ref[...] = y


def _conv_ffn(x, g, w_in, w_conv, b_conv, w_down, g_final, final_norm, n_seq, seq_len, past=None):
    t, d = x.shape
    d_ff = w_down.shape[0]
    assert d_ff % FFN_FB == 0 and CONV_W == 3
    carry_state = past is None
    tm = min(FFN_TM if carry_state else FFN_TM // 2, t)
    assert t % tm == 0
    row = lambda w: pl.BlockSpec((tm, w), lambda i: (i, 0))
    in_specs = [row(d), _resident((1, d)), _resident(w_in.shape), _resident(w_conv.shape), _resident((1, d_ff)),
                _resident(w_down.shape), _resident((1, d))]
    args = [x, g, w_in, w_conv, b_conv, w_down, g_final]
    scratch = [pltpu.VMEM((tm, d_ff), BF16)]
    if carry_state:
        assert seq_len % tm == 0
        tiles_per_seq = seq_len // tm
        out_specs = [row(d), pl.BlockSpec((None, CONV_W - 1, d_ff), lambda i: (i // tiles_per_seq, 0, 0))]
        out_shape = [jax.ShapeDtypeStruct(x.shape, F32), jax.ShapeDtypeStruct((n_seq, CONV_W - 1, d_ff), F32)]
        scratch.append(pltpu.VMEM((CONV_W - 1, d_ff), F32))
    else:
        assert seq_len == SUBLANE
        tiles_per_seq = 0
        pad = jnp.zeros((n_seq, seq_len, d_ff), F32)
        p1 = pad.at[:, 0].set(past[:, 1]).reshape(t, d_ff)
        p2 = pad.at[:, 0:2].set(past).reshape(t, d_ff)
        in_specs += [row(d_ff), row(d_ff)]
        args += [p1, p2]
        out_specs = [row(d), row(d_ff)]
        out_shape = [jax.ShapeDtypeStruct(x.shape, F32), jax.ShapeDtypeStruct((t, d_ff), F32)]
    return pl.pallas_call(
        functools.partial(_ffn_kernel, carry_state, final_norm, tiles_per_seq, seq_len),
        grid=(t // tm,),
        in_specs=in_specs,
        out_specs=out_specs,
        out_shape=out_shape,
        scratch_shapes=scratch,
        compiler_params=_params("arbitrary"),
        name="conv_ffn",
    )(*args)


def _pack_w_in(w):
    d = w.shape[0]
    sizes = _in_sizes(d)
    offs, o = {}, 0
    for n in IN_NAMES:
        offs[n] = (o, o + sizes[n])
        o += sizes[n]
    seg = lambda n: w[:, offs[n][0]:offs[n][1]]
    tail = jnp.zeros((d, LANE), w.dtype)
    tail = tail.at[:, TAIL_LR:TAIL_LR + GLA_LOWRANK].set(seg("glr"))
    tail = tail.at[:, TAIL_FF:TAIL_FF + ATT_HEADS].set(seg("ff"))
    return jnp.concatenate([seg(n) for n in PACK_ORDER] + [tail], axis=1).astype(BF16)


def kernel(x_prompt, x_sample, state_gla, cache_sb_k, cache_sb_v, cache_fox_k, cache_fox_v, cache_fox_logf,
           state_ffn_conv, page_table, g_mix, w_in, w_gla_lr, b_gla_lr, g_gla_out, b_fox_f, w_br_gla, w_br_sb,
           w_br_fox, w_o, g_ffn, w_ffn_in, w_conv, b_conv, w_down, g_final):
    bp, lp, d = x_prompt.shape
    bs, ls, _ = x_sample.shape
    depth = w_in.shape[0]
    d_ff = w_down.shape[1]
    n_pool, page = cache_sb_k.shape[1], cache_sb_k.shape[2]
    tp, ts = bp * lp, bs * ls
    xp = x_prompt.reshape(tp, d)
    xs = x_sample.reshape(ts, d)
    page_t = lambda c: jnp.transpose(c, (0, 1, 3, 4, 2)).reshape(depth, n_pool, ATT_W, page)
    c_sb_k, c_sb_v, c_fx_k, c_fx_v = (page_t(c) for c in (cache_sb_k, cache_sb_v, cache_fox_k, cache_fox_v))
    c_lf_t = jnp.swapaxes(cache_fox_logf, 2, 3)
    state_t = jnp.swapaxes(state_gla, 3, 4)
    g_fin = g_final.reshape(1, d)

    outs_p, outs_s = [], []
    for l in range(depth):
        w_packed = _pack_w_in(w_in[l])
        wlr = jnp.zeros((LANE, GLA_KW), F32).at[TAIL_LR:TAIL_LR + GLA_LOWRANK].set(w_gla_lr[l])
        wlr_hi = wlr.astype(BF16)
        wlr_lo = (wlr - wlr_hi.astype(F32)).astype(BF16)
        blr = b_gla_lr[l].reshape(1, GLA_KW)
        bff = jnp.zeros((1, LANE), F32).at[0, TAIL_FF:TAIL_FF + ATT_HEADS].set(b_fox_f[l])
        g_mix_l = g_mix[l].reshape(1, d)
        g_out = g_gla_out[l].reshape(1, GLA_DV)
        w_g, w_s, w_f, w_o_l = (w.astype(BF16) for w in (w_br_gla[l], w_br_sb[l], w_br_fox[l], w_o[l]))
        ffn_w = (g_ffn[l].reshape(1, d), w_ffn_in[l].astype(BF16), w_conv[l], b_conv[l].reshape(1, d_ff),
                 w_down[l].astype(BF16), g_fin)
        final = l == depth - 1

        pp = _project(xp, g_mix_l, w_packed, wlr_hi, wlr_lo, blr, bff, bp, lp, True)
        logf_p = pp["tail"][:, TAIL_FF:TAIL_FF + ATT_HEADS]
        ckp, cqa = _seq_cumsum(pp["tail"], bp, lp)
        o_sb, o_fx = _prompt_attention(pp["sqm"], pp["skb"], pp["svtb"], pp["fqm"], cqa, pp["fkb"], ckp, pp["fvtb"],
                                       bp, lp)
        o_gla, st_p = _gla(pp["gq"], pp["gk"], pp["gv"], pp["la"], pp["gg"], g_out, bp, lp)
        xp = _merge(xp, o_gla, o_sb, o_fx, pp["gates"], w_g, w_s, w_f, w_o_l)
        xp, conv_p = _conv_ffn(xp, *ffn_w, final, bp, lp)
        rows_p = [jnp.transpose(pp[n].reshape(bp, ATT_HEADS, ATT_DH, lp), (0, 3, 1, 2))
                  for n in ("skt", "svt", "fkt", "fvt")]
        outs_p.append((jnp.swapaxes(st_p, 2, 3), *rows_p, logf_p, conv_p))

        ps = _project(xs, g_mix_l, w_packed, wlr_hi, wlr_lo, blr, bff, bs, ls, False)
        logf_s = ps["tail"][:, TAIL_FF:TAIL_FF + ATT_HEADS]
        lfn_t = jnp.swapaxes(logf_s.reshape(bs, ls, ATT_HEADS), 1, 2)
        lfn_t = jnp.pad(lfn_t, ((0, 0), (0, 0), (0, LANE - ls)))
        o_sb, o_fx = _decode_attention(l, page_table, ps["sq"], ps["fq"], ps["sk"], ps["sv"], ps["fk"], ps["fv"],
                                       lfn_t, c_sb_k, c_sb_v, c_fx_k, c_fx_v, c_lf_t)
        o_gla, st_s = _gla(ps["gq"], ps["gk"], ps["gv"], ps["la"], ps["gg"], g_out, bs, ls, state_t[l])
        xs = _merge(xs, o_gla, o_sb, o_fx, ps["gates"], w_g, w_s, w_f, w_o_l)
        xs, u_s = _conv_ffn(xs, *ffn_w, final, bs, ls, state_ffn_conv[l])
        conv_s = u_s.reshape(bs, ls, d_ff)[:, ls - (CONV_W - 1):]
        outs_s.append((jnp.swapaxes(st_s, 2, 3), ps["sk"], ps["sv"], ps["fk"], ps["fv"], logf_s, conv_s))

    def stack(outs, i, shape):
        return jnp.stack([o[i].reshape(shape) for o in outs])

    kv_p, kv_s = (bp, lp, ATT_HEADS, ATT_DH), (bs, ls, ATT_HEADS, ATT_DH)
    st_shape_p, st_shape_s = (bp, GLA_HEADS, GLA_DK, GLA_DV), (bs, GLA_HEADS, GLA_DK, GLA_DV)
    return (xp.reshape(bp, lp, d), xs.reshape(bs, ls, d),
            stack(outs_p, 0, st_shape_p), stack(outs_s, 0, st_shape_s),
            stack(outs_p, 1, kv_p), stack(outs_p, 2, kv_p), stack(outs_s, 1, kv_s), stack(outs_s, 2, kv_s),
            stack(outs_p, 3, kv_p), stack(outs_p, 4, kv_p), stack(outs_p, 5, (bp, lp, ATT_HEADS)),
            stack(outs_s, 3, kv_s), stack(outs_s, 4, kv_s), stack(outs_s, 5, (bs, ls, ATT_HEADS)),
            stack(outs_p, 6, (bp, CONV_W - 1, d_ff)), stack(outs_s, 6, (bs, CONV_W - 1, d_ff)))
```
